```python
import math
import jax
import jax.numpy as jnp
from jax import lax
import numpy as np

D_MODEL = 4096
BATCH = 16
SEQ = 256
DEPTH = 2
DEC_BATCH = 2
DEC_SEQ = 4096
PAST_LEN = 512

GRID_W = 64
MIX_W = D_MODEL
GROUP_W = MIX_W // 4
S5_WIDTH = GROUP_W
S5_CH_PER_GROUP = 16
S5_GROUPS = S5_WIDTH // S5_CH_PER_GROUP
S5_STATE = 64
DN_HEADS = 8
DN_HEAD_DIM = GROUP_W // DN_HEADS
DN_CONV = 3
DN_CHUNK = 64
GQA_HEADS = 8
GQA_KV_HEADS = 2
GQA_HEAD_DIM = GROUP_W // GQA_HEADS
DIFF_HEADS = 8
DIFF_V_DIM = GROUP_W // DIFF_HEADS
DIFF_QK_DIM = DIFF_V_DIM // 2
Q_BLOCK = 128
ROPE_THETA = 10000.0
PEER_HEADS = 8
PEER_KEY_DIM = 128
PEER_N_KEYS = 128
PEER_EXPERTS = PEER_N_KEYS * PEER_N_KEYS
PEER_TOPK = 16
PEER_TOKEN_BLOCK = 64
N_MOD = 6
EPS = 1e-6
IN_SPLITS = (S5_WIDTH,
             GROUP_W, GROUP_W, GROUP_W, GROUP_W, 2 * DN_HEADS, 2 * DN_HEADS,
             GQA_HEADS * GQA_HEAD_DIM, GQA_KV_HEADS * GQA_HEAD_DIM, GQA_KV_HEADS * GQA_HEAD_DIM,
             DIFF_HEADS * 2 * DIFF_QK_DIM, DIFF_HEADS * 2 * DIFF_QK_DIM, DIFF_HEADS * DIFF_V_DIM)
IN_COLS = sum(IN_SPLITS)

kernel_name = 'hybrid_prefix_diffusion_trunk'


def rmsnorm(x, g):
    xf = x.astype(jnp.float32)
    y = xf * lax.rsqrt(jnp.mean(xf * xf, axis=-1, keepdims=True) + EPS)
    return (y * g.astype(jnp.float32)).astype(x.dtype)


def l2norm(x):
    return x * lax.rsqrt(jnp.sum(x * x, axis=-1, keepdims=True) + EPS)


def split_columns(z, sizes):
    out, start = [], 0
    for s in sizes:
        out.append(z[..., start:start + s])
        start += s
    return out


def grid_rope_tables(length, dim):
    n_rows = length // GRID_W
    row = jnp.repeat(jnp.arange(n_rows), GRID_W).astype(jnp.float32)
    col = jnp.tile(jnp.arange(GRID_W), n_rows).astype(jnp.float32)
    quarter = dim // 4
    freqs = ROPE_THETA ** (-jnp.arange(quarter, dtype=jnp.float32) / quarter)
    ang = jnp.concatenate([row[:, None] * freqs, col[:, None] * freqs], axis=-1)
    return jnp.cos(ang), jnp.sin(ang)


def apply_rope(x, cos, sin):
    half = x.shape[-1] // 2
    shape = (1, cos.shape[0]) + (1,) * (x.ndim - 3) + (half,)
    c, s = cos.reshape(shape), sin.reshape(shape)
    x1, x2 = x[..., :half], x[..., half:]
    return jnp.concatenate([x1 * c - x2 * s, x2 * c + x1 * s], axis=-1).astype(x.dtype)


def sweep_query_blocks(q, fn):
    B, Lq = q.shape[:2]
    nb = Lq // Q_BLOCK
    qb = jnp.swapaxes(q.reshape((B, nb, Q_BLOCK) + q.shape[2:]), 0, 1)
    out = lax.map(fn, qb)
    return jnp.swapaxes(out, 0, 1).reshape((B, Lq) + out.shape[3:])


def _cmul(ar, ai, br, bi):
    return ar * br - ai * bi, ar * bi + ai * br


def s5_direction(u, h0, lam_re, lam_im, log_dt, b_re, b_im, c_re, c_im):
    lam_re = lam_re.astype(jnp.float32)
    lam_im = lam_im.astype(jnp.float32)
    dt = jnp.exp(log_dt.astype(jnp.float32))[:, None]
    mag = jnp.exp(lam_re * dt)
    ab_re, ab_im = mag * jnp.cos(lam_im * dt), mag * jnp.sin(lam_im * dt)
    den = lam_re * lam_re + lam_im * lam_im
    nr = ab_re - 1.0
    coef_re = (nr * lam_re + ab_im * lam_im) / den
    coef_im = (ab_im * lam_re - nr * lam_im) / den
    bb_re, bb_im = _cmul(coef_re[..., None], coef_im[..., None],
                         b_re.astype(jnp.float32), b_im.astype(jnp.float32))
    bu_re = jnp.einsum('gph,blgh->blgp', bb_re, u)
    bu_im = jnp.einsum('gph,blgh->blgp', bb_im, u)
    if h0 is not None:
        i_re, i_im = _cmul(ab_re, ab_im, h0[0], h0[1])
        bu_re = bu_re.at[:, 0].add(i_re)
        bu_im = bu_im.at[:, 0].add(i_im)
    a_re = jnp.broadcast_to(ab_re, bu_re.shape)
    a_im = jnp.broadcast_to(ab_im, bu_im.shape)

    def combine(e1, e2):
        a1r, a1i, b1r, b1i = e1
        a2r, a2i, b2r, b2i = e2
        ar, ai = _cmul(a2r, a2i, a1r, a1i)
        br, bi = _cmul(a2r, a2i, b1r, b1i)
        return ar, ai, br + b2r, bi + b2i

    _, _, h_re, h_im = lax.associative_scan(combine, (a_re, a_im, bu_re, bu_im), axis=1)
    y = (jnp.einsum('ghp,blgp->blgh', c_re.astype(jnp.float32), h_re)
         - jnp.einsum('ghp,blgp->blgh', c_im.astype(jnp.float32), h_im))
    return y, h_re[:, -1], h_im[:, -1]


def s5_mixer(u, p, h0):
    B, L, _ = u.shape
    uf = u.astype(jnp.float32).reshape(B, L, S5_GROUPS, S5_CH_PER_GROUP)
    ys, fin_re, fin_im = [], [], []
    for d in range(2):
        ud = uf if d == 0 else jnp.flip(uf, 1)
        init = None if h0 is None else (h0[0][:, d].astype(jnp.float32), h0[1][:, d].astype(jnp.float32))
        y, h_re, h_im = s5_direction(ud, init, p['s5_lambda_re'][d], p['s5_lambda_im'][d], p['s5_log_dt'][d],
                                     p['s5_b_re'][d], p['s5_b_im'][d], p['s5_c_re'][d], p['s5_c_im'][d])
        ys.append(y if d == 0 else jnp.flip(y, 1))
        fin_re.append(h_re)
        fin_im.append(h_im)
    y = ys[0] + ys[1] + p['s5_d'].astype(jnp.float32).reshape(S5_GROUPS, S5_CH_PER_GROUP) * uf
    y = jax.nn.gelu(y.reshape(B, L, S5_WIDTH))
    y = y * jax.nn.sigmoid(y @ p['s5_w_glu'].astype(jnp.float32))
    return y, jnp.stack(fin_re, axis=1), jnp.stack(fin_im, axis=1)


def short_conv(x, w):
    width = w.shape[0]
    y = lax.conv_general_dilated(x, w.astype(x.dtype)[:, None, :], window_strides=(1,),
                                 padding=[(width // 2, width // 2)],
                                 dimension_numbers=('NWC', 'WIO', 'NWC'),
                                 feature_group_count=x.shape[-1])
    return jax.nn.silu(y)


def chunk_gated_delta(q, k, v, g, beta, s0):
    B, L, H, dk = q.shape
    dv = v.shape[-1]
    n = L // DN_CHUNK

    def chunks(t):
        t = t.reshape((B, n, DN_CHUNK) + t.shape[2:])
        return jnp.moveaxis(t, (1, 3), (0, 2))

    qc, kc, vc = chunks(q), chunks(k), chunks(v)
    gc = jnp.cumsum(chunks(g), axis=-1)
    bc = chunks(beta)
    idx = jnp.arange(DN_CHUNK)
    tril = idx[:, None] >= idx[None, :]
    strict = idx[:, None] > idx[None, :]
    diff = gc[..., :, None] - gc[..., None, :]
    decay = jnp.where(tril, jnp.exp(jnp.where(tril, diff, 0.0)), 0.0)
    kb = kc * bc[..., None]
    a = jnp.where(strict, jnp.einsum('...id,...jd->...ij', kb, kc) * decay, 0.0)
    eye = jnp.eye(DN_CHUNK, dtype=jnp.float32)
    rhs = jnp.concatenate([vc * bc[..., None], kb * jnp.exp(gc)[..., None]], axis=-1)
    sol = lax.linalg.triangular_solve(a + eye, rhs, left_side=True, lower=True)
    u_c, w_c = sol[..., :dv], sol[..., dv:]

    def step(s, inp):
        q_i, k_i, u_i, w_i, g_i, dec_i = inp
        v_new = u_i - jnp.einsum('bhcd,bhde->bhce', w_i, s)
        attn = jnp.einsum('bhid,bhjd->bhij', q_i, k_i) * dec_i
        o = (jnp.einsum('bhcd,bhde->bhce', q_i * jnp.exp(g_i)[..., None], s)
             + jnp.einsum('bhij,bhje->bhie', attn, v_new))
        g_last = g_i[..., -1]
        s = (s * jnp.exp(g_last)[..., None, None]
             + jnp.einsum('bhcd,bhce->bhde', k_i * jnp.exp(g_last[..., None] - g_i)[..., None], v_new))
        return s, o

    s_fin, o = lax.scan(step, s0, (qc, kc, u_c, w_c, gc, decay))
    o = jnp.moveaxis(o, (0, 2), (1, 3)).reshape(B, L, H, dv)
    return o, s_fin


def deltanet_mixer(q, k, v, gate, a, b, p, s0):
    B, L, _ = q.shape
    qkv = short_conv(jnp.concatenate([q, k, v], axis=-1), p['dn_conv_w'])
    qkv = qkv.astype(jnp.float32).reshape(B, L, 3, DN_HEADS, DN_HEAD_DIM)
    qh = l2norm(qkv[:, :, 0]) * (DN_HEAD_DIM ** -0.5)
    kh = l2norm(qkv[:, :, 1])
    vh = qkv[:, :, 2]
    a = a.astype(jnp.float32).reshape(B, L, 2, DN_HEADS)
    b = b.astype(jnp.float32).reshape(B, L, 2, DN_HEADS)
    g = -jnp.exp(p['dn_a_log'].astype(jnp.float32)) * jax.nn.softplus(a + p['dn_dt_bias'].astype(jnp.float32))
    beta = jax.nn.sigmoid(b)
    if s0 is None:
        s0 = jnp.zeros((B, 2, DN_HEADS, DN_HEAD_DIM, DN_HEAD_DIM), jnp.float32)
    s0 = s0.astype(jnp.float32)
    outs, finals = [], []
    for d in range(2):
        seq = (qh, kh, vh, g[:, :, d], beta[:, :, d])
        if d == 1:
            seq = tuple(jnp.flip(t, 1) for t in seq)
        o, s_fin = chunk_gated_delta(*seq, s0[:, d])
        outs.append(o if d == 0 else jnp.flip(o, 1))
        finals.append(s_fin)
    o = rmsnorm(outs[0] + outs[1], p['dn_norm_g']) * jax.nn.silu(
        gate.astype(jnp.float32).reshape(B, L, DN_HEADS, DN_HEAD_DIM))
    return o.reshape(B, L, GROUP_W), jnp.stack(finals, axis=1)


def gqa_mixer(q, k, v, p, rope, ctx_kv):
    B, L, _ = q.shape
    q = rmsnorm(q.reshape(B, L, GQA_HEADS, GQA_HEAD_DIM), p['gqa_q_norm'])
    k = rmsnorm(k.reshape(B, L, GQA_KV_HEADS, GQA_HEAD_DIM), p['gqa_k_norm'])
    v = v.reshape(B, L, GQA_KV_HEADS, GQA_HEAD_DIM)
    if rope is not None:
        q = apply_rope(q, rope[0], rope[1])
        k = apply_rope(k, rope[0], rope[1])
    if ctx_kv is None:
        keys, vals = k, v
    else:
        keys = jnp.concatenate([ctx_kv[0].astype(k.dtype), k], axis=1)
        vals = jnp.concatenate([ctx_kv[1].astype(v.dtype), v], axis=1)
    kf, vf = keys.astype(jnp.float32), vals.astype(jnp.float32)
    scale = GQA_HEAD_DIM ** -0.5
    qg = q.astype(jnp.float32).reshape(B, L, GQA_KV_HEADS, GQA_HEADS // GQA_KV_HEADS, GQA_HEAD_DIM)

    def block(qi):
        s = jnp.einsum('bqhgd,bkhd->bhgqk', qi, kf) * scale
        pm = jax.nn.softmax(s, axis=-1)
        return jnp.einsum('bhgqk,bkhd->bqhgd', pm, vf)

    o = sweep_query_blocks(qg, block)
    return o.reshape(B, L, GROUP_W), k, v


def diff_mixer(q, k, v, p, layer_idx, rope, ctx_kv):
    B, L, _ = q.shape
    lam_init = 0.8 - 0.6 * math.exp(-0.3 * layer_idx)
    q = q.reshape(B, L, DIFF_HEADS, 2, DIFF_QK_DIM)
    k = k.reshape(B, L, DIFF_HEADS, 2, DIFF_QK_DIM)
    v = v.reshape(B, L, DIFF_HEADS, DIFF_V_DIM)
    if rope is not None:
        q = apply_rope(q, rope[0], rope[1])
        k = apply_rope(k, rope[0], rope[1])
    if ctx_kv is None:
        keys, vals = k, v
    else:
        keys = jnp.concatenate([ctx_kv[0].astype(k.dtype), k], axis=1)
        vals = jnp.concatenate([ctx_kv[1].astype(v.dtype), v], axis=1)
    kf, vf = keys.astype(jnp.float32), vals.astype(jnp.float32)
    lp = p['diff_lambda'].astype(jnp.float32)
    lam = jnp.exp(jnp.sum(lp[0] * lp[1])) - jnp.exp(jnp.sum(lp[2] * lp[3])) + lam_init
    scale = DIFF_QK_DIM ** -0.5

    def block(qi):
        s = jnp.einsum('bqhcd,bkhcd->bhcqk', qi, kf) * scale
        pm = jax.nn.softmax(s, axis=-1)
        w = pm[:, :, 0] - lam * pm[:, :, 1]
        return jnp.einsum('bhqk,bkhe->bqhe', w, vf)

    o = sweep_query_blocks(q.astype(jnp.float32), block)
    o = rmsnorm(o, p['diff_subln_g']) * (1.0 - lam_init)
    return o.reshape(B, L, GROUP_W), k, v


def peer_ffn(x, p):
    B, L, D = x.shape
    q = (x @ p['peer_w_q']).astype(jnp.float32).reshape(B, L, PEER_HEADS, 2, PEER_KEY_DIM // 2)
    s = jnp.einsum('blhcd,hcnd->blhcn', q, p['peer_keys'].astype(jnp.float32))
    sv, si = lax.top_k(s, PEER_TOPK)
    cand = (sv[..., 0, :, None] + sv[..., 1, None, :]).reshape(B, L, PEER_HEADS, PEER_TOPK * PEER_TOPK)
    cidx = (si[..., 0, :, None] * PEER_N_KEYS + si[..., 1, None, :]).reshape(B, L, PEER_HEADS, PEER_TOPK * PEER_TOPK)
    fv, fpos = lax.top_k(cand, PEER_TOPK)
    eidx = jnp.take_along_axis(cidx, fpos, axis=-1)
    gates = jax.nn.softmax(fv, axis=-1)
    T = B * L
    nb = T // PEER_TOKEN_BLOCK
    xt = x.reshape(nb, PEER_TOKEN_BLOCK, D)
    it = eidx.reshape(nb, PEER_TOKEN_BLOCK, PEER_HEADS * PEER_TOPK)
    gt = gates.reshape(nb, PEER_TOKEN_BLOCK, PEER_HEADS * PEER_TOPK)
    u_tab, v_tab = p['peer_u'], p['peer_v']

    def block(args):
        xb, ib, gb = args
        hid = jax.nn.gelu(jnp.einsum('td,tkd->tk', xb, u_tab[ib]).astype(jnp.float32))
        return jnp.einsum('tk,tkd->td', hid * gb, v_tab[ib].astype(jnp.float32))

    out = lax.map(block, (xt, it, gt))
    return out.reshape(B, L, D)


def trunk_layer(x, mod, p, layer_idx, rope, cache):
    is_ctx = cache is None
    shift1, scale1, gate1, shift2, scale2, gate2 = jnp.split(mod[:, None, :], N_MOD, axis=-1)
    h = rmsnorm(x, p['norm1_g']) * (1.0 + scale1) + shift1
    z = h @ p['w_in']
    (s5_u, dn_q, dn_k, dn_v, dn_gate, dn_a, dn_b, gq, gk, gv, fq, fk, fv) = split_columns(z, IN_SPLITS)
    y_a, s5_re, s5_im = s5_mixer(s5_u, p, None if is_ctx else (cache['s5_re'], cache['s5_im']))
    y_b, delta = deltanet_mixer(dn_q, dn_k, dn_v, dn_gate, dn_a, dn_b, p, None if is_ctx else cache['delta'])
    rope_g, rope_d = (None, None) if is_ctx else rope
    y_c, gqa_k, gqa_v = gqa_mixer(gq, gk, gv, p, rope_g, None if is_ctx else (cache['gqa_k'], cache['gqa_v']))
    y_d, diff_k, diff_v = diff_mixer(fq, fk, fv, p, layer_idx, rope_d,
                                     None if is_ctx else (cache['diff_k'], cache['diff_v']))
    y = jnp.concatenate([y_a, y_b, y_c, y_d], axis=-1) @ p['w_out']
    x = x + gate1 * y
    h2 = rmsnorm(x, p['norm2_g']) * (1.0 + scale2) + shift2
    x = x + gate2 * peer_ffn(h2, p)
    if is_ctx:
        return x, (gqa_k, gqa_v, diff_k, diff_v, s5_re, s5_im, delta)
    return x, None


def setup_inputs(seed: int = 0) -> dict:
    key = jax.random.key(seed)
    ks = iter(jax.random.split(key, 64))

    def nrm(shape, scale):
        return scale * jax.random.normal(next(ks), shape, jnp.float32)

    def unif(shape, lo, hi):
        return jax.random.uniform(next(ks), shape, jnp.float32, lo, hi)

    D = D_MODEL
    dt_dn = jnp.exp(unif((DEPTH, 2, DN_HEADS), math.log(1e-3), math.log(1e-1)))
    inp = {}
    inp['x_prompt'] = nrm((BATCH, SEQ, D), 1.0)
    inp['x_sample'] = nrm((DEC_BATCH, DEC_SEQ, D), 1.0)
    inp['c'] = nrm((DEC_BATCH, D), 1.0)
    inp['cache_gqa_k'] = nrm((DEC_BATCH, DEPTH, PAST_LEN, GQA_KV_HEADS, GQA_HEAD_DIM), 1.0)
    inp['cache_gqa_v'] = nrm((DEC_BATCH, DEPTH, PAST_LEN, GQA_KV_HEADS, GQA_HEAD_DIM), 1.0)
    inp['cache_diff_k'] = nrm((DEC_BATCH, DEPTH, PAST_LEN, DIFF_HEADS, 2, DIFF_QK_DIM), 1.0)
    inp['cache_diff_v'] = nrm((DEC_BATCH, DEPTH, PAST_LEN, DIFF_HEADS, DIFF_V_DIM), 1.0)
    inp['state_s5_re'] = nrm((DEC_BATCH, DEPTH, 2, S5_GROUPS, S5_STATE), 0.1)
    inp['state_s5_im'] = nrm((DEC_BATCH, DEPTH, 2, S5_GROUPS, S5_STATE), 0.1)
    inp['state_delta'] = nrm((DEC_BATCH, DEPTH, 2, DN_HEADS, DN_HEAD_DIM, DN_HEAD_DIM), 0.1)
    inp['c_ctx'] = nrm((D,), 1.0)
    inp['w_mod'] = nrm((DEPTH, D, N_MOD * D), 0.5 * D ** -0.5)
    inp['b_mod'] = nrm((DEPTH, N_MOD * D), 0.01)
    inp['norm1_g'] = 1.0 + nrm((DEPTH, D), 0.05)
    inp['norm2_g'] = 1.0 + nrm((DEPTH, D), 0.05)
    inp['w_in'] = nrm((DEPTH, D, IN_COLS), D ** -0.5)
    inp['s5_lambda_re'] = -0.5 + nrm((DEPTH, 2, S5_GROUPS, S5_STATE), 0.01)
    inp['s5_lambda_im'] = math.pi * jnp.arange(S5_STATE, dtype=jnp.float32) + nrm((DEPTH, 2, S5_GROUPS, S5_STATE), 0.01)
    inp['s5_log_dt'] = unif((DEPTH, 2, S5_GROUPS), math.log(1e-3), math.log(1e-1))
    inp['s5_b_re'] = nrm((DEPTH, 2, S5_GROUPS, S5_STATE, S5_CH_PER_GROUP), (2.0 * S5_CH_PER_GROUP) ** -0.5)
    inp['s5_b_im'] = nrm((DEPTH, 2, S5_GROUPS, S5_STATE, S5_CH_PER_GROUP), (2.0 * S5_CH_PER_GROUP) ** -0.5)
    inp['s5_c_re'] = nrm((DEPTH, 2, S5_GROUPS, S5_CH_PER_GROUP, S5_STATE), (2.0 * S5_STATE) ** -0.5)
    inp['s5_c_im'] = nrm((DEPTH, 2, S5_GROUPS, S5_CH_PER_GROUP, S5_STATE), (2.0 * S5_STATE) ** -0.5)
    inp['s5_d'] = nrm((DEPTH, S5_WIDTH), 1.0)
    inp['s5_w_glu'] = nrm((DEPTH, S5_WIDTH, S5_WIDTH), S5_WIDTH ** -0.5)
    inp['dn_conv_w'] = nrm((DEPTH, DN_CONV, 3 * GROUP_W), DN_CONV ** -0.5)
    inp['dn_a_log'] = jnp.log(unif((DEPTH, 2, DN_HEADS), 1.0, 16.0))
    inp['dn_dt_bias'] = dt_dn + jnp.log(-jnp.expm1(-dt_dn))
    inp['dn_norm_g'] = 1.0 + nrm((DEPTH, DN_HEAD_DIM), 0.05)
    inp['gqa_q_norm'] = 1.0 + nrm((DEPTH, GQA_HEAD_DIM), 0.05)
    inp['gqa_k_norm'] = 1.0 + nrm((DEPTH, GQA_HEAD_DIM), 0.05)
    inp['diff_lambda'] = nrm((DEPTH, 4, DIFF_QK_DIM), 0.1)
    inp['diff_subln_g'] = 1.0 + nrm((DEPTH, DIFF_V_DIM), 0.05)
    inp['w_out'] = nrm((DEPTH, MIX_W, D), MIX_W ** -0.5)
    inp['peer_w_q'] = nrm((DEPTH, D, PEER_HEADS * PEER_KEY_DIM), D ** -0.5)
    inp['peer_keys'] = nrm((DEPTH, PEER_HEADS, 2, PEER_N_KEYS, PEER_KEY_DIM // 2), (PEER_KEY_DIM // 2) ** -0.5)
    inp['peer_u'] = nrm((DEPTH, PEER_EXPERTS, D), D ** -0.5)
    inp['peer_v'] = nrm((DEPTH, PEER_EXPERTS, D), 1.0)
    inp['final_norm_g'] = 1.0 + nrm((D,), 0.05)
    return inp


def reference(x_prompt, x_sample, c, cache_gqa_k, cache_gqa_v, cache_diff_k, cache_diff_v,
              state_s5_re, state_s5_im, state_delta, c_ctx, w_mod, b_mod, norm1_g, norm2_g, w_in,
              s5_lambda_re, s5_lambda_im, s5_log_dt, s5_b_re, s5_b_im, s5_c_re, s5_c_im, s5_d, s5_w_glu,
              dn_conv_w, dn_a_log, dn_dt_bias, dn_norm_g, gqa_q_norm, gqa_k_norm, diff_lambda, diff_subln_g,
              w_out, peer_w_q, peer_keys, peer_u, peer_v, final_norm_g):
    lat_len = x_sample.shape[1]
    rope = (grid_rope_tables(lat_len, GQA_HEAD_DIM), grid_rope_tables(lat_len, DIFF_QK_DIM))
    cond_ctx = jax.nn.silu(c_ctx.astype(jnp.float32))[None]
    cond_lat = jax.nn.silu(c.astype(jnp.float32))
    h_ctx, h_lat = x_prompt, x_sample
    new_gk, new_gv, new_dk, new_dv, new_sre, new_sim, new_delta = [], [], [], [], [], [], []
    for l in range(DEPTH):
        p = {'norm1_g': norm1_g[l], 'norm2_g': norm2_g[l], 'w_in': w_in[l],
             's5_lambda_re': s5_lambda_re[l], 's5_lambda_im': s5_lambda_im[l], 's5_log_dt': s5_log_dt[l],
             's5_b_re': s5_b_re[l], 's5_b_im': s5_b_im[l], 's5_c_re': s5_c_re[l], 's5_c_im': s5_c_im[l],
             's5_d': s5_d[l], 's5_w_glu': s5_w_glu[l],
             'dn_conv_w': dn_conv_w[l], 'dn_a_log': dn_a_log[l], 'dn_dt_bias': dn_dt_bias[l],
             'dn_norm_g': dn_norm_g[l], 'gqa_q_norm': gqa_q_norm[l], 'gqa_k_norm': gqa_k_norm[l],
             'diff_lambda': diff_lambda[l], 'diff_subln_g': diff_subln_g[l], 'w_out': w_out[l],
             'peer_w_q': peer_w_q[l], 'peer_keys': peer_keys[l], 'peer_u': peer_u[l], 'peer_v': peer_v[l]}
        mod_ctx = cond_ctx @ w_mod[l] + b_mod[l]
        mod_lat = cond_lat @ w_mod[l] + b_mod[l]
        h_ctx, ctx_out = trunk_layer(h_ctx, mod_ctx, p, l, None, None)
        cache_l = {'gqa_k': cache_gqa_k[:, l], 'gqa_v': cache_gqa_v[:, l],
                   'diff_k': cache_diff_k[:, l], 'diff_v': cache_diff_v[:, l],
                   's5_re': state_s5_re[:, l], 's5_im': state_s5_im[:, l], 'delta': state_delta[:, l]}
        h_lat, _ = trunk_layer(h_lat, mod_lat, p, l, rope, cache_l)
        new_gk.append(ctx_out[0])
        new_gv.append(ctx_out[1])
        new_dk.append(ctx_out[2])
        new_dv.append(ctx_out[3])
        new_sre.append(ctx_out[4])
        new_sim.append(ctx_out[5])
        new_delta.append(ctx_out[6])
    y_prompt = rmsnorm(h_ctx, final_norm_g)
    y_sample = rmsnorm(h_lat, final_norm_g)
    return (y_prompt, y_sample, jnp.stack(new_gk, axis=1), jnp.stack(new_gv, axis=1),
            jnp.stack(new_dk, axis=1), jnp.stack(new_dv, axis=1), jnp.stack(new_sre, axis=1),
            jnp.stack(new_sim, axis=1), jnp.stack(new_delta, axis=1))
```

```python
import functools
import math

import jax
import jax.numpy as jnp
from jax import lax
from jax.experimental import pallas as pl
from jax.experimental.pallas import tpu as pltpu

F32 = jnp.float32
BF16 = jnp.bfloat16

D_MODEL = 4096
GRID_W = 64
GROUP_W = D_MODEL // 4
S5_WIDTH = GROUP_W
S5_CH_PER_GROUP = 16
S5_GROUPS = S5_WIDTH // S5_CH_PER_GROUP
S5_STATE = 64
DN_HEADS = 8
DN_HEAD_DIM = GROUP_W // DN_HEADS
DN_CHUNK = 64
GQA_HEADS = 8
GQA_KV_HEADS = 2
GQA_HEAD_DIM = GROUP_W // GQA_HEADS
DIFF_HEADS = 8
DIFF_V_DIM = GROUP_W // DIFF_HEADS
DIFF_QK_DIM = DIFF_V_DIM // 2
Q_BLOCK = 128
ROPE_THETA = 10000.0
PEER_HEADS = 8
PEER_KEY_DIM = 128
PEER_N_KEYS = 128
PEER_TOPK = 16
N_MOD = 6
EPS = 1e-6

_REF_SPLITS = (('s5_u', 1024), ('dn_q', 1024), ('dn_k', 1024), ('dn_v', 1024), ('dn_gate', 1024),
               ('dn_a', 16), ('dn_b', 16), ('gq', 1024), ('gk', 256), ('gv', 256),
               ('fq', 1024), ('fk', 1024), ('fv', 1024))
_Z_ORDER = ('s5_u', 'dn_q', 'dn_k', 'dn_v', 'dn_gate', 'gq', 'fq', 'fk', 'fv', 'gk', 'gv', 'dn_a', 'dn_b')
LANES = 128
VMEM_LIMIT = 56 * 1024 * 1024


def _z_layout():
    widths = dict(_REF_SPLITS)
    ref_start, s = {}, 0
    for name, w in _REF_SPLITS:
        ref_start[name] = s
        s += w
    z_start, s = {}, 0
    for name in _Z_ORDER:
        z_start[name] = s
        s += widths[name]
    total = -(-s // LANES) * LANES
    return widths, ref_start, z_start, total


Z_WIDTH, Z_REF_START, Z_START, Z_COLS = _z_layout()


def _cparams(sem):
    return pltpu.CompilerParams(dimension_semantics=sem, vmem_limit_bytes=VMEM_LIMIT)


def _silu(x):
    return x * (1.0 / (1.0 + jnp.exp(-x)))


def _gelu_tanh(x):
    return 0.5 * x * (1.0 + jnp.tanh(math.sqrt(2.0 / math.pi) * (x + 0.044715 * (x * x * x))))


def _mod_kernel(cond_ref, w_ref, b_ref, o_ref):
    a = _silu(cond_ref[...]).astype(BF16)
    o_ref[0] = jnp.dot(a, w_ref[0].astype(BF16), preferred_element_type=F32) + b_ref[0]


def modulation(cond, w_mod, b_mod, tn=512):
    depth, d, n = w_mod.shape
    return pl.pallas_call(
        _mod_kernel,
        out_shape=jax.ShapeDtypeStruct((depth, 8, n), F32),
        grid=(depth, n // tn),
        in_specs=[pl.BlockSpec((8, d), lambda l, j: (0, 0)),
                  pl.BlockSpec((1, d, tn), lambda l, j: (l, 0, j)),
                  pl.BlockSpec((1, 1, tn), lambda l, j: (l, 0, j))],
        out_specs=pl.BlockSpec((1, 8, tn), lambda l, j: (l, 0, j)),
        compiler_params=_cparams(("arbitrary", "arbitrary")),
        name="modulation",
    )(cond, w_mod, b_mod.reshape(depth, 1, n))


def _mod_row(i, tile, n_ctx_tok, lat_len):
    t0 = i * tile
    return jnp.where(t0 < n_ctx_tok, 0, 1 + (t0 - n_ctx_tok) // lat_len)


def _norm_mod(x, g, shift, scale):
    r = lax.rsqrt(jnp.mean(x * x, axis=-1, keepdims=True) + EPS)
    return (x * r * g) * (1.0 + scale) + shift


def _inproj_kernel(x_ref, g_ref, mod_ref, w_ref, o_ref, h_ref):
    @pl.when(pl.program_id(1) == 0)
    def _():
        m = mod_ref[0]
        h_ref[...] = _norm_mod(x_ref[...], g_ref[...], m[0:1], m[1:2]).astype(BF16)

    o_ref[...] = jnp.dot(h_ref[...], w_ref[...], preferred_element_type=F32).astype(o_ref.dtype)


def in_projection(x, g, mod, w, n_ctx_tok, lat_len, tm=512, tn=896):
    t, d = x.shape
    n = w.shape[1]
    row = functools.partial(_mod_row, tile=tm, n_ctx_tok=n_ctx_tok, lat_len=lat_len)
    return pl.pallas_call(
        _inproj_kernel,
        out_shape=jax.ShapeDtypeStruct((t, n), F32),
        grid=(t // tm, n // tn),
        in_specs=[pl.BlockSpec((tm, d), lambda i, j: (i, 0)),
                  pl.BlockSpec((1, d), lambda i, j: (0, 0)),
                  pl.BlockSpec((1, N_MOD, d), lambda i, j: (row(i), 0, 0)),
                  pl.BlockSpec((d, tn), lambda i, j: (0, j))],
        out_specs=pl.BlockSpec((tm, tn), lambda i, j: (i, j)),
        scratch_shapes=[pltpu.VMEM((tm, d), BF16)],
        compiler_params=_cparams(("arbitrary", "arbitrary")),
        name="in_projection",
    )(x, g.reshape(1, d), mod, w)


def _outproj_kernel(y_ref, w_ref, x_ref, mod_ref, o_ref):
    acc = jnp.dot(y_ref[...], w_ref[...], preferred_element_type=F32)
    o_ref[...] = x_ref[...] + mod_ref[0][2:3] * acc


def out_projection(y, w, x, mod, n_ctx_tok, lat_len, tm=512, tn=1024):
    t, kdim = y.shape
    n = w.shape[1]
    row = functools.partial(_mod_row, tile=tm, n_ctx_tok=n_ctx_tok, lat_len=lat_len)
    return pl.pallas_call(
        _outproj_kernel,
        out_shape=jax.ShapeDtypeStruct((t, n), F32),
        grid=(t // tm, n // tn),
        in_specs=[pl.BlockSpec((tm, kdim), lambda i, j: (i, 0)),
                  pl.BlockSpec((kdim, tn), lambda i, j: (0, j)),
                  pl.BlockSpec((tm, tn), lambda i, j: (i, j)),
                  pl.BlockSpec((1, N_MOD, tn), lambda i, j: (row(i), 0, j))],
        out_specs=pl.BlockSpec((tm, tn), lambda i, j: (i, j)),
        compiler_params=_cparams(("arbitrary", "arbitrary")),
        name="out_projection",
    )(y, w, x, mod)


def _top_rows(x, k):
    vals = []
    cur = x
    for _ in range(k):
        mx = jnp.max(cur, axis=0, keepdims=True)
        vals.append(mx)
        cur = jnp.where(cur == mx, -jnp.inf, cur)
    return jnp.concatenate(vals, axis=0)


def _pair_sums(a, b):
    k, cols = a.shape
    sub = 8
    pad_rows = -(-k // sub) * sub - k
    neg = jnp.full((pad_rows, cols), -jnp.inf, F32)
    a_pad = jnp.concatenate([a, neg], axis=0)
    b_pad = jnp.concatenate([b, neg], axis=0)
    pieces = [a[0:1] + b_pad]
    row = lax.broadcasted_iota(jnp.int32, (sub, cols), 0)
    for i in range(1, sub):
        pieces.append(jnp.where(row < k // (i + 1), a[i:i + 1] + b_pad[0:sub], -jnp.inf))
    pieces.append(a_pad[sub:] + b[0:1])
    return jnp.concatenate(pieces, axis=0)


def _peer_query_kernel(x_ref, g_ref, mod_ref, wq_ref, keys_ref,
                       h2_ref, s2_ref, e2_ref, thr_ref, e1_ref):
    m = mod_ref[0]
    hb = _norm_mod(x_ref[...], g_ref[...], m[3:4], m[4:5]).astype(BF16)
    h2_ref[...] = hb
    q = jnp.dot(hb, wq_ref[...], preferred_element_type=F32)
    nt = (((1,), (1,)), ((), ()))
    for hd in range(PEER_HEADS):
        qh = q[:, hd * PEER_KEY_DIM:(hd + 1) * PEER_KEY_DIM].astype(BF16)
        s1 = lax.dot_general(keys_ref[2 * hd], qh, nt, preferred_element_type=F32)
        s2 = lax.dot_general(keys_ref[2 * hd + 1], qh, nt, preferred_element_type=F32)
        a = _top_rows(s1, PEER_TOPK + 1)
        b = _top_rows(s2, PEER_TOPK + 1)
        v = _top_rows(_pair_sums(a, b), PEER_TOPK + 1)
        z = jnp.sum(jnp.exp(v[:PEER_TOPK] - v[0:1]), axis=0, keepdims=True)
        tau = 0.5 * (v[PEER_TOPK - 1:PEER_TOPK] + v[PEER_TOPK:PEER_TOPK + 1])
        s2_ref[hd] = s2
        e2_ref[hd] = jnp.exp(s2 - b[0:1])
        thr_ref[hd] = tau - s1
        e1_ref[hd] = jnp.exp(s1 - a[0:1]) / z


def peer_query(x, g, mod, wq, keys_pad, n_ctx_tok, lat_len, tq=256):
    t, d = x.shape
    row = functools.partial(_mod_row, tile=tq, n_ctx_tok=n_ctx_tok, lat_len=lat_len)
    aux = jax.ShapeDtypeStruct((PEER_HEADS, PEER_N_KEYS, t), F32)
    aux_spec = pl.BlockSpec((PEER_HEADS, PEER_N_KEYS, tq), lambda i: (0, 0, i))
    return pl.pallas_call(
        _peer_query_kernel,
        out_shape=(jax.ShapeDtypeStruct((t, d), BF16), aux, aux, aux, aux),
        grid=(t // tq,),
        in_specs=[pl.BlockSpec((tq, d), lambda i: (i, 0)),
                  pl.BlockSpec((1, d), lambda i: (0, 0)),
                  pl.BlockSpec((1, N_MOD, d), lambda i: (row(i), 0, 0)),
                  pl.BlockSpec(wq.shape, lambda i: (0, 0)),
                  pl.BlockSpec(keys_pad.shape, lambda i: (0, 0, 0))],
        out_specs=(pl.BlockSpec((tq, d), lambda i: (i, 0)), aux_spec, aux_spec, aux_spec, aux_spec),
        compiler_params=_cparams(("arbitrary",)),
        name="peer_query",
    )(x, g.reshape(1, d), mod, wq, keys_pad)


def _peer_main_kernel(x_ref, u_ref, vt_ref, s2_ref, e2_ref, thr_ref, e1_ref, o_ref, *, n_sub):
    k = pl.program_id(1)

    @pl.when(k == 0)
    def _():
        o_ref[...] = jnp.zeros_like(o_ref)

    nt = (((1,), (1,)), ((), ()))
    hid = _gelu_tanh(lax.dot_general(u_ref[...], x_ref[...], nt, preferred_element_type=F32))
    tt = hid.shape[1]
    parts = []
    for r in range(n_sub):
        i1 = k * n_sub + r
        w = jnp.zeros((PEER_N_KEYS, tt), F32)
        for hd in range(PEER_HEADS):
            thr = thr_ref[hd, pl.ds(i1, 1), :]
            e1 = e1_ref[hd, pl.ds(i1, 1), :]
            w = w + jnp.where(s2_ref[hd] >= thr, e2_ref[hd], 0.0) * e1
        parts.append((hid[r * PEER_N_KEYS:(r + 1) * PEER_N_KEYS] * w).astype(BF16))
    hw = jnp.concatenate(parts, axis=0) if n_sub > 1 else parts[0]
    o_ref[...] += jnp.dot(vt_ref[...], hw, preferred_element_type=F32)


def peer_main(h2, u, vt, s2, e2, thr, e1, tt=512, te=256):
    t, d = h2.shape
    n_exp = u.shape[0]
    aux_spec = pl.BlockSpec((PEER_HEADS, PEER_N_KEYS, tt), lambda i, k: (0, 0, i))
    return pl.pallas_call(
        functools.partial(_peer_main_kernel, n_sub=te // PEER_N_KEYS),
        out_shape=jax.ShapeDtypeStruct((d, t), F32),
        grid=(t // tt, n_exp // te),
        in_specs=[pl.BlockSpec((tt, d), lambda i, k: (i, 0)),
                  pl.BlockSpec((te, d), lambda i, k: (k, 0)),
                  pl.BlockSpec((d, te), lambda i, k: (0, k)),
                  aux_spec, aux_spec, aux_spec, aux_spec],
        out_specs=pl.BlockSpec((d, tt), lambda i, k: (0, i)),
        compiler_params=_cparams(("arbitrary", "arbitrary")),
        name="peer_main",
    )(h2, u, vt, s2, e2, thr, e1)


def _peer_residual_kernel(x_ref, pt_ref, mod_ref, g_ref, o_ref, *, final):
    x = x_ref[...] + mod_ref[0][5:6] * pt_ref[...].T
    if final:
        r = lax.rsqrt(jnp.mean(x * x, axis=-1, keepdims=True) + EPS)
        x = x * r * g_ref[...]
    o_ref[...] = x


def peer_residual(x, peer_t, mod, g_final, n_ctx_tok, lat_len, final, tr=256):
    t, d = x.shape
    row = functools.partial(_mod_row, tile=tr, n_ctx_tok=n_ctx_tok, lat_len=lat_len)
    return pl.pallas_call(
        functools.partial(_peer_residual_kernel, final=final),
        out_shape=jax.ShapeDtypeStruct((t, d), F32),
        grid=(t // tr,),
        in_specs=[pl.BlockSpec((tr, d), lambda i: (i, 0)),
                  pl.BlockSpec((d, tr), lambda i: (0, i)),
                  pl.BlockSpec((1, N_MOD, d), lambda i: (row(i), 0, 0)),
                  pl.BlockSpec((1, d), lambda i: (0, 0))],
        out_specs=pl.BlockSpec((tr, d), lambda i: (i, 0)),
        compiler_params=_cparams(("arbitrary",)),
        name="peer_residual",
    )(x, peer_t, mod, g_final.reshape(1, d))


def _rmsnorm(x, g):
    xf = x.astype(F32)
    y = xf * lax.rsqrt(jnp.mean(xf * xf, axis=-1, keepdims=True) + EPS)
    return (y * g.astype(F32)).astype(x.dtype)


def _l2norm(x):
    return x * lax.rsqrt(jnp.sum(x * x, axis=-1, keepdims=True) + EPS)


def _rope_tables(length, dim):
    n_rows = length // GRID_W
    row = jnp.repeat(jnp.arange(n_rows), GRID_W).astype(F32)
    col = jnp.tile(jnp.arange(GRID_W), n_rows).astype(F32)
    quarter = dim // 4
    freqs = ROPE_THETA ** (-jnp.arange(quarter, dtype=F32) / quarter)
    ang = jnp.concatenate([row[:, None] * freqs, col[:, None] * freqs], axis=-1)
    return jnp.cos(ang), jnp.sin(ang)


def _apply_rope(x, cos, sin):
    half = x.shape[-1] // 2
    shape = (1, cos.shape[0]) + (1,) * (x.ndim - 3) + (half,)
    c, s = cos.reshape(shape), sin.reshape(shape)
    x1, x2 = x[..., :half], x[..., half:]
    return jnp.concatenate([x1 * c - x2 * s, x2 * c + x1 * s], axis=-1).astype(x.dtype)


def _sweep_query_blocks(q, fn):
    b, lq = q.shape[:2]
    nb = lq // Q_BLOCK
    qb = jnp.swapaxes(q.reshape((b, nb, Q_BLOCK) + q.shape[2:]), 0, 1)
    out = lax.map(fn, qb)
    return jnp.swapaxes(out, 0, 1).reshape((b, lq) + out.shape[3:])


def _cmul(ar, ai, br, bi):
    return ar * br - ai * bi, ar * bi + ai * br


def _s5_direction(u, h0, lam_re, lam_im, log_dt, b_re, b_im, c_re, c_im):
    dt = jnp.exp(log_dt)[:, None]
    mag = jnp.exp(lam_re * dt)
    ab_re, ab_im = mag * jnp.cos(lam_im * dt), mag * jnp.sin(lam_im * dt)
    den = lam_re * lam_re + lam_im * lam_im
    nr = ab_re - 1.0
    coef_re = (nr * lam_re + ab_im * lam_im) / den
    coef_im = (ab_im * lam_re - nr * lam_im) / den
    bb_re, bb_im = _cmul(coef_re[..., None], coef_im[..., None], b_re, b_im)
    bu_re = jnp.einsum('gph,blgh->blgp', bb_re, u)
    bu_im = jnp.einsum('gph,blgh->blgp', bb_im, u)
    if h0 is not None:
        i_re, i_im = _cmul(ab_re, ab_im, h0[0], h0[1])
        bu_re = bu_re.at[:, 0].add(i_re)
        bu_im = bu_im.at[:, 0].add(i_im)
    a_re = jnp.broadcast_to(ab_re, bu_re.shape)
    a_im = jnp.broadcast_to(ab_im, bu_im.shape)

    def combine(e1, e2):
        a1r, a1i, b1r, b1i = e1
        a2r, a2i, b2r, b2i = e2
        ar, ai = _cmul(a2r, a2i, a1r, a1i)
        br, bi = _cmul(a2r, a2i, b1r, b1i)
        return ar, ai, br + b2r, bi + b2i

    _, _, h_re, h_im = lax.associative_scan(combine, (a_re, a_im, bu_re, bu_im), axis=1)
    y = jnp.einsum('ghp,blgp->blgh', c_re, h_re) - jnp.einsum('ghp,blgp->blgh', c_im, h_im)
    return y, h_re[:, -1], h_im[:, -1]


def _s5_mixer(u, p, h0):
    b, l, _ = u.shape
    uf = u.reshape(b, l, S5_GROUPS, S5_CH_PER_GROUP)
    ys, fin_re, fin_im = [], [], []
    for d in range(2):
        ud = uf if d == 0 else jnp.flip(uf, 1)
        init = None if h0 is None else (h0[0][:, d], h0[1][:, d])
        y, h_re, h_im = _s5_direction(ud, init, p['s5_lambda_re'][d], p['s5_lambda_im'][d], p['s5_log_dt'][d],
                                      p['s5_b_re'][d], p['s5_b_im'][d], p['s5_c_re'][d], p['s5_c_im'][d])
        ys.append(y if d == 0 else jnp.flip(y, 1))
        fin_re.append(h_re)
        fin_im.append(h_im)
    y = ys[0] + ys[1] + p['s5_d'].reshape(S5_GROUPS, S5_CH_PER_GROUP) * uf
    y = jax.nn.gelu(y.reshape(b, l, S5_WIDTH))
    y = y * jax.nn.sigmoid(y @ p['s5_w_glu'])
    return y, jnp.stack(fin_re, axis=1), jnp.stack(fin_im, axis=1)


def _short_conv(x, w):
    width = w.shape[0]
    y = lax.conv_general_dilated(x, w[:, None, :], window_strides=(1,),
                                 padding=[(width // 2, width // 2)],
                                 dimension_numbers=('NWC', 'WIO', 'NWC'),
                                 feature_group_count=x.shape[-1])
    return jax.nn.silu(y)


def _chunk_gated_delta(q, k, v, g, beta, s0):
    b, l, h, dk = q.shape
    dv = v.shape[-1]
    n = l // DN_CHUNK

    def chunks(t):
        t = t.reshape((b, n, DN_CHUNK) + t.shape[2:])
        return jnp.moveaxis(t, (1, 3), (0, 2))

    qc, kc, vc = chunks(q), chunks(k), chunks(v)
    gc = jnp.cumsum(chunks(g), axis=-1)
    bc = chunks(beta)
    idx = jnp.arange(DN_CHUNK)
    tril = idx[:, None] >= idx[None, :]
    strict = idx[:, None] > idx[None, :]
    diff = gc[..., :, None] - gc[..., None, :]
    decay = jnp.where(tril, jnp.exp(jnp.where(tril, diff, 0.0)), 0.0)
    kb = kc * bc[..., None]
    a = jnp.where(strict, jnp.einsum('...id,...jd->...ij', kb, kc) * decay, 0.0)
    eye = jnp.eye(DN_CHUNK, dtype=F32)
    rhs = jnp.concatenate([vc * bc[..., None], kb * jnp.exp(gc)[..., None]], axis=-1)
    sol = lax.linalg.triangular_solve(a + eye, rhs, left_side=True, lower=True)
    u_c, w_c = sol[..., :dv], sol[..., dv:]

    def step(s, inp):
        q_i, k_i, u_i, w_i, g_i, dec_i = inp
        v_new = u_i - jnp.einsum('bhcd,bhde->bhce', w_i, s)
        attn = jnp.einsum('bhid,bhjd->bhij', q_i, k_i) * dec_i
        o = (jnp.einsum('bhcd,bhde->bhce', q_i * jnp.exp(g_i)[..., None], s)
             + jnp.einsum('bhij,bhje->bhie', attn, v_new))
        g_last = g_i[..., -1]
        s = (s * jnp.exp(g_last)[..., None, None]
             + jnp.einsum('bhcd,bhce->bhde', k_i * jnp.exp(g_last[..., None] - g_i)[..., None], v_new))
        return s, o

    s_fin, o = lax.scan(step, s0, (qc, kc, u_c, w_c, gc, decay))
    o = jnp.moveaxis(o, (0, 2), (1, 3)).reshape(b, l, h, dv)
    return o, s_fin


def _deltanet_mixer(q, k, v, gate, a, bb, p, s0):
    b, l, _ = q.shape
    qkv = _short_conv(jnp.concatenate([q, k, v], axis=-1), p['dn_conv_w'])
    qkv = qkv.reshape(b, l, 3, DN_HEADS, DN_HEAD_DIM)
    qh = _l2norm(qkv[:, :, 0]) * (DN_HEAD_DIM ** -0.5)
    kh = _l2norm(qkv[:, :, 1])
    vh = qkv[:, :, 2]
    a = a.reshape(b, l, 2, DN_HEADS)
    bb = bb.reshape(b, l, 2, DN_HEADS)
    g = -jnp.exp(p['dn_a_log']) * jax.nn.softplus(a + p['dn_dt_bias'])
    beta = jax.nn.sigmoid(bb)
    if s0 is None:
        s0 = jnp.zeros((b, 2, DN_HEADS, DN_HEAD_DIM, DN_HEAD_DIM), F32)
    outs, finals = [], []
    for d in range(2):
        seq = (qh, kh, vh, g[:, :, d], beta[:, :, d])
        if d == 1:
            seq = tuple(jnp.flip(t, 1) for t in seq)
        o, s_fin = _chunk_gated_delta(*seq, s0[:, d])
        outs.append(o if d == 0 else jnp.flip(o, 1))
        finals.append(s_fin)
    o = _rmsnorm(outs[0] + outs[1], p['dn_norm_g']) * jax.nn.silu(gate.reshape(b, l, DN_HEADS, DN_HEAD_DIM))
    return o.reshape(b, l, GROUP_W), jnp.stack(finals, axis=1)


def _gqa_mixer(q, k, v, p, rope, ctx_kv):
    b, l, _ = q.shape
    q = _rmsnorm(q.reshape(b, l, GQA_HEADS, GQA_HEAD_DIM), p['gqa_q_norm'])
    k = _rmsnorm(k.reshape(b, l, GQA_KV_HEADS, GQA_HEAD_DIM), p['gqa_k_norm'])
    v = v.reshape(b, l, GQA_KV_HEADS, GQA_HEAD_DIM)
    if rope is not None:
        q = _apply_rope(q, rope[0], rope[1])
        k = _apply_rope(k, rope[0], rope[1])
    if ctx_kv is None:
        keys, vals = k, v
    else:
        keys = jnp.concatenate([ctx_kv[0], k], axis=1)
        vals = jnp.concatenate([ctx_kv[1], v], axis=1)
    scale = GQA_HEAD_DIM ** -0.5
    qg = q.reshape(b, l, GQA_KV_HEADS, GQA_HEADS // GQA_KV_HEADS, GQA_HEAD_DIM)

    def block(qi):
        s = jnp.einsum('bqhgd,bkhd->bhgqk', qi, keys) * scale
        pm = jax.nn.softmax(s, axis=-1)
        return jnp.einsum('bhgqk,bkhd->bqhgd', pm, vals)

    o = _sweep_query_blocks(qg, block)
    return o.reshape(b, l, GROUP_W), k, v


def _diff_mixer(q, k, v, p, layer_idx, rope, ctx_kv):
    b, l, _ = q.shape
    lam_init = 0.8 - 0.6 * math.exp(-0.3 * layer_idx)
    q = q.reshape(b, l, DIFF_HEADS, 2, DIFF_QK_DIM)
    k = k.reshape(b, l, DIFF_HEADS, 2, DIFF_QK_DIM)
    v = v.reshape(b, l, DIFF_HEADS, DIFF_V_DIM)
    if rope is not None:
        q = _apply_rope(q, rope[0], rope[1])
        k = _apply_rope(k, rope[0], rope[1])
    if ctx_kv is None:
        keys, vals = k, v
    else:
        keys = jnp.concatenate([ctx_kv[0], k], axis=1)
        vals = jnp.concatenate([ctx_kv[1], v], axis=1)
    lp = p['diff_lambda']
    lam = jnp.exp(jnp.sum(lp[0] * lp[1])) - jnp.exp(jnp.sum(lp[2] * lp[3])) + lam_init
    scale = DIFF_QK_DIM ** -0.5

    def block(qi):
        s = jnp.einsum('bqhcd,bkhcd->bhcqk', qi, keys) * scale
        pm = jax.nn.softmax(s, axis=-1)
        w = pm[:, :, 0] - lam * pm[:, :, 1]
        return jnp.einsum('bhqk,bkhe->bqhe', w, vals)

    o = _sweep_query_blocks(q, block)
    o = _rmsnorm(o, p['diff_subln_g']) * (1.0 - lam_init)
    return o.reshape(b, l, GROUP_W), k, v


def _mixers(z, p, layer_idx, rope, cache):
    is_ctx = cache is None
    col = lambda name: z[..., Z_START[name]:Z_START[name] + Z_WIDTH[name]]
    y_a, s5_re, s5_im = _s5_mixer(col('s5_u'), p, None if is_ctx else (cache['s5_re'], cache['s5_im']))
    y_b, delta = _deltanet_mixer(col('dn_q'), col('dn_k'), col('dn_v'), col('dn_gate'), col('dn_a'), col('dn_b'),
                                 p, None if is_ctx else cache['delta'])
    rope_g, rope_d = (None, None) if is_ctx else rope
    y_c, gqa_k, gqa_v = _gqa_mixer(col('gq'), col('gk'), col('gv'), p, rope_g,
                                   None if is_ctx else (cache['gqa_k'], cache['gqa_v']))
    y_d, diff_k, diff_v = _diff_mixer(col('fq'), col('fk'), col('fv'), p, layer_idx, rope_d,
                                      None if is_ctx else (cache['diff_k'], cache['diff_v']))
    y = jnp.concatenate([y_a, y_b, y_c, y_d], axis=-1)
    return y, (gqa_k, gqa_v, diff_k, diff_v, s5_re, s5_im, delta)


def _permute_w_in(w):
    cols = [w[:, Z_REF_START[n]:Z_REF_START[n] + Z_WIDTH[n]] for n in _Z_ORDER]
    used = sum(Z_WIDTH[n] for n in _Z_ORDER)
    cols.append(jnp.zeros((w.shape[0], Z_COLS - used), w.dtype))
    return jnp.concatenate(cols, axis=1).astype(BF16)


def _pad_keys(keys):
    h, two, n, half = keys.shape
    z = jnp.zeros((h, n, half), keys.dtype)
    k0 = jnp.concatenate([keys[:, 0], z], axis=-1)
    k1 = jnp.concatenate([z, keys[:, 1]], axis=-1)
    return jnp.stack([k0, k1], axis=1).reshape(2 * h, n, 2 * half).astype(BF16)


def kernel(x_prompt, x_sample, c, cache_gqa_k, cache_gqa_v, cache_diff_k, cache_diff_v, state_s5_re, state_s5_im, state_delta, c_ctx, w_mod, b_mod, norm1_g, norm2_g, w_in, s5_lambda_re, s5_lambda_im, s5_log_dt, s5_b_re, s5_b_im, s5_c_re, s5_c_im, s5_d, s5_w_glu, dn_conv_w, dn_a_log, dn_dt_bias, dn_norm_g, gqa_q_norm, gqa_k_norm, diff_lambda, diff_subln_g, w_out, peer_w_q, peer_keys, peer_u, peer_v, final_norm_g):
    batch, seq, d = x_prompt.shape
    dec_batch, lat_len, _ = x_sample.shape
    depth = w_in.shape[0]
    n_ctx_tok = batch * seq
    rope = (_rope_tables(lat_len, GQA_HEAD_DIM), _rope_tables(lat_len, DIFF_QK_DIM))

    cond = jnp.concatenate([c_ctx[None], c, jnp.zeros((8 - 1 - dec_batch, d), F32)], axis=0)
    mod_all = modulation(cond, w_mod, b_mod).reshape(depth, 8, N_MOD, d)

    x = jnp.concatenate([x_prompt.reshape(n_ctx_tok, d), x_sample.reshape(dec_batch * lat_len, d)], axis=0)
    new_state = [[] for _ in range(7)]
    for l in range(depth):
        p = {'s5_lambda_re': s5_lambda_re[l], 's5_lambda_im': s5_lambda_im[l], 's5_log_dt': s5_log_dt[l],
             's5_b_re': s5_b_re[l], 's5_b_im': s5_b_im[l], 's5_c_re': s5_c_re[l], 's5_c_im': s5_c_im[l],
             's5_d': s5_d[l], 's5_w_glu': s5_w_glu[l],
             'dn_conv_w': dn_conv_w[l], 'dn_a_log': dn_a_log[l], 'dn_dt_bias': dn_dt_bias[l],
             'dn_norm_g': dn_norm_g[l], 'gqa_q_norm': gqa_q_norm[l], 'gqa_k_norm': gqa_k_norm[l],
             'diff_lambda': diff_lambda[l], 'diff_subln_g': diff_subln_g[l]}
        mod = mod_all[l]
        z = in_projection(x, norm1_g[l], mod, _permute_w_in(w_in[l]), n_ctx_tok, lat_len)
        z_ctx = z[:n_ctx_tok].reshape(batch, seq, Z_COLS)
        z_lat = z[n_ctx_tok:].reshape(dec_batch, lat_len, Z_COLS)
        y_ctx, ctx_out = _mixers(z_ctx, p, l, None, None)
        cache_l = {'gqa_k': cache_gqa_k[:, l], 'gqa_v': cache_gqa_v[:, l],
                   'diff_k': cache_diff_k[:, l], 'diff_v': cache_diff_v[:, l],
                   's5_re': state_s5_re[:, l], 's5_im': state_s5_im[:, l], 'delta': state_delta[:, l]}
        y_lat, _ = _mixers(z_lat, p, l, rope, cache_l)
        y = jnp.concatenate([y_ctx.reshape(n_ctx_tok, -1), y_lat.reshape(dec_batch * lat_len, -1)], axis=0)
        x = out_projection(y.astype(BF16), w_out[l].astype(BF16), x, mod, n_ctx_tok, lat_len)
        h2, s2, e2, thr, e1 = peer_query(x, norm2_g[l], mod, peer_w_q[l].astype(BF16), _pad_keys(peer_keys[l]),
                                         n_ctx_tok, lat_len)
        peer_t = peer_main(h2, peer_u[l].astype(BF16), peer_v[l].astype(BF16).T, s2, e2, thr, e1)
        x = peer_residual(x, peer_t, mod, final_norm_g, n_ctx_tok, lat_len, final=(l == depth - 1))
        for acc, val in zip(new_state, ctx_out):
            acc.append(val)
    y_prompt = x[:n_ctx_tok].reshape(batch, seq, d)
    y_sample = x[n_ctx_tok:].reshape(dec_batch, lat_len, d)
    return (y_prompt, y_sample) + tuple(jnp.stack(s, axis=1) for s in new_state)
```

```python
import functools
import math

import jax
import jax.numpy as jnp
from jax import lax
from jax.experimental import pallas as pl
from jax.experimental.pallas import tpu as pltpu

F32 = jnp.float32
BF16 = jnp.bfloat16

D_MODEL = 4096
GRID_W = 64
GROUP_W = D_MODEL // 4
S5_WIDTH = GROUP_W
S5_CH_PER_GROUP = 16
S5_GROUPS = S5_WIDTH // S5_CH_PER_GROUP
S5_STATE = 64
DN_HEADS = 8
DN_HEAD_DIM = GROUP_W // DN_HEADS
DN_CHUNK = 64
GQA_HEADS = 8
GQA_KV_HEADS = 2
GQA_HEAD_DIM = GROUP_W // GQA_HEADS
DIFF_HEADS = 8
DIFF_V_DIM = GROUP_W // DIFF_HEADS
DIFF_QK_DIM = DIFF_V_DIM // 2
Q_BLOCK = 128
ROPE_THETA = 10000.0
PEER_HEADS = 8
PEER_KEY_DIM = 128
PEER_N_KEYS = 128
PEER_TOPK = 16
N_MOD = 6
EPS = 1e-6

_REF_SPLITS = (('s5_u', 1024), ('dn_q', 1024), ('dn_k', 1024), ('dn_v', 1024), ('dn_gate', 1024),
               ('dn_a', 16), ('dn_b', 16), ('gq', 1024), ('gk', 256), ('gv', 256),
               ('fq', 1024), ('fk', 1024), ('fv', 1024))
_Z_ORDER = ('s5_u', 'dn_q', 'dn_k', 'dn_v', 'dn_gate', 'gq', 'fq', 'fk', 'fv', 'gk', 'gv', 'dn_a', 'dn_b')
LANES = 128
VMEM_LIMIT = 56 * 1024 * 1024


def _z_layout():
    widths = dict(_REF_SPLITS)
    ref_start, s = {}, 0
    for name, w in _REF_SPLITS:
        ref_start[name] = s
        s += w
    z_start, s = {}, 0
    for name in _Z_ORDER:
        z_start[name] = s
        s += widths[name]
    total = -(-s // LANES) * LANES
    return widths, ref_start, z_start, total


Z_WIDTH, Z_REF_START, Z_START, Z_COLS = _z_layout()


def _cparams(sem):
    return pltpu.CompilerParams(dimension_semantics=sem, vmem_limit_bytes=VMEM_LIMIT)


def _silu(x):
    return x * (1.0 / (1.0 + jnp.exp(-x)))


def _gelu_tanh(x):
    return 0.5 * x * (1.0 + jnp.tanh(math.sqrt(2.0 / math.pi) * (x + 0.044715 * (x * x * x))))


def _mod_kernel(cond_ref, w_ref, b_ref, o_ref):
    a = _silu(cond_ref[...]).astype(BF16)
    o_ref[0] = jnp.dot(a, w_ref[0].astype(BF16), preferred_element_type=F32) + b_ref[0]


def modulation(cond, w_mod, b_mod, tn=512):
    depth, d, n = w_mod.shape
    return pl.pallas_call(
        _mod_kernel,
        out_shape=jax.ShapeDtypeStruct((depth, 8, n), F32),
        grid=(depth, n // tn),
        in_specs=[pl.BlockSpec((8, d), lambda l, j: (0, 0)),
                  pl.BlockSpec((1, d, tn), lambda l, j: (l, 0, j)),
                  pl.BlockSpec((1, 1, tn), lambda l, j: (l, 0, j))],
        out_specs=pl.BlockSpec((1, 8, tn), lambda l, j: (l, 0, j)),
        compiler_params=_cparams(("arbitrary", "arbitrary")),
        name="modulation",
    )(cond, w_mod, b_mod.reshape(depth, 1, n))


def _mod_row(i, tile, n_ctx_tok, lat_len):
    t0 = i * tile
    return jnp.where(t0 < n_ctx_tok, 0, 1 + (t0 - n_ctx_tok) // lat_len)


def _norm_mod(x, g, shift, scale):
    r = lax.rsqrt(jnp.mean(x * x, axis=-1, keepdims=True) + EPS)
    return (x * r * g) * (1.0 + scale) + shift


def _inproj_kernel(x_ref, g_ref, mod_ref, w_ref, o_ref, h_ref):
    @pl.when(pl.program_id(1) == 0)
    def _():
        m = mod_ref[0]
        h_ref[...] = _norm_mod(x_ref[...], g_ref[...], m[0:1], m[1:2]).astype(BF16)

    o_ref[...] = jnp.dot(h_ref[...], w_ref[...], preferred_element_type=F32).astype(o_ref.dtype)


def in_projection(x, g, mod, w, n_ctx_tok, lat_len, tm=512, tn=896):
    t, d = x.shape
    n = w.shape[1]
    row = functools.partial(_mod_row, tile=tm, n_ctx_tok=n_ctx_tok, lat_len=lat_len)
    return pl.pallas_call(
        _inproj_kernel,
        out_shape=jax.ShapeDtypeStruct((t, n), F32),
        grid=(t // tm, n // tn),
        in_specs=[pl.BlockSpec((tm, d), lambda i, j: (i, 0)),
                  pl.BlockSpec((1, d), lambda i, j: (0, 0)),
                  pl.BlockSpec((1, N_MOD, d), lambda i, j: (row(i), 0, 0)),
                  pl.BlockSpec((d, tn), lambda i, j: (0, j))],
        out_specs=pl.BlockSpec((tm, tn), lambda i, j: (i, j)),
        scratch_shapes=[pltpu.VMEM((tm, d), BF16)],
        compiler_params=_cparams(("arbitrary", "arbitrary")),
        name="in_projection",
    )(x, g.reshape(1, d), mod, w)


def _outproj_kernel(ya_ref, yb_ref, yc_ref, yd_ref, w_ref, x_ref, mod_ref, o_ref):
    acc = None
    for m, y_ref in enumerate((ya_ref, yb_ref, yc_ref, yd_ref)):
        part = jnp.dot(y_ref[...], w_ref[m * GROUP_W:(m + 1) * GROUP_W, :], preferred_element_type=F32)
        acc = part if acc is None else acc + part
    o_ref[...] = x_ref[...] + mod_ref[0][2:3] * acc


def out_projection(ys, w, x, mod, n_ctx_tok, lat_len, tm=512, tn=1024):
    t, n = x.shape
    kdim = w.shape[0]
    row = functools.partial(_mod_row, tile=tm, n_ctx_tok=n_ctx_tok, lat_len=lat_len)
    y_spec = pl.BlockSpec((tm, GROUP_W), lambda i, j: (i, 0))
    return pl.pallas_call(
        _outproj_kernel,
        out_shape=jax.ShapeDtypeStruct((t, n), F32),
        grid=(t // tm, n // tn),
        in_specs=[y_spec, y_spec, y_spec, y_spec,
                  pl.BlockSpec((kdim, tn), lambda i, j: (0, j)),
                  pl.BlockSpec((tm, tn), lambda i, j: (i, j)),
                  pl.BlockSpec((1, N_MOD, tn), lambda i, j: (row(i), 0, j))],
        out_specs=pl.BlockSpec((tm, tn), lambda i, j: (i, j)),
        compiler_params=_cparams(("arbitrary", "arbitrary")),
        name="out_projection",
    )(*ys, w, x, mod)


def _top_rows(x, k):
    vals = []
    cur = x
    for _ in range(k):
        mx = jnp.max(cur, axis=0, keepdims=True)
        vals.append(mx)
        cur = jnp.where(cur == mx, -jnp.inf, cur)
    return jnp.concatenate(vals, axis=0)


def _pair_sums(a, b):
    k, cols = a.shape
    sub = 8
    pad_rows = -(-k // sub) * sub - k
    neg = jnp.full((pad_rows, cols), -jnp.inf, F32)
    a_pad = jnp.concatenate([a, neg], axis=0)
    b_pad = jnp.concatenate([b, neg], axis=0)
    pieces = [a[0:1] + b_pad]
    row = lax.broadcasted_iota(jnp.int32, (sub, cols), 0)
    for i in range(1, sub):
        pieces.append(jnp.where(row < k // (i + 1), a[i:i + 1] + b_pad[0:sub], -jnp.inf))
    pieces.append(a_pad[sub:] + b[0:1])
    return jnp.concatenate(pieces, axis=0)


def _peer_query_kernel(x_ref, g_ref, mod_ref, wq_ref, keys_ref,
                       h2_ref, s2_ref, e2_ref, thr_ref, e1_ref):
    m = mod_ref[0]
    hb = _norm_mod(x_ref[...], g_ref[...], m[3:4], m[4:5]).astype(BF16)
    h2_ref[...] = hb
    q = jnp.dot(hb, wq_ref[...], preferred_element_type=F32)
    nt = (((1,), (1,)), ((), ()))
    for hd in range(PEER_HEADS):
        qh = q[:, hd * PEER_KEY_DIM:(hd + 1) * PEER_KEY_DIM].astype(BF16)
        s1 = lax.dot_general(keys_ref[2 * hd], qh, nt, preferred_element_type=F32)
        s2 = lax.dot_general(keys_ref[2 * hd + 1], qh, nt, preferred_element_type=F32)
        a = _top_rows(s1, PEER_TOPK + 1)
        b = _top_rows(s2, PEER_TOPK + 1)
        v = _top_rows(_pair_sums(a, b), PEER_TOPK + 1)
        z = jnp.sum(jnp.exp(v[:PEER_TOPK] - v[0:1]), axis=0, keepdims=True)
        tau = 0.5 * (v[PEER_TOPK - 1:PEER_TOPK] + v[PEER_TOPK:PEER_TOPK + 1])
        s2_ref[hd] = s2
        e2_ref[hd] = jnp.exp(s2 - b[0:1])
        thr_ref[hd] = tau - s1
        e1_ref[hd] = jnp.exp(s1 - a[0:1]) / z


def peer_query(x, g, mod, wq, keys_pad, n_ctx_tok, lat_len, tq=256):
    t, d = x.shape
    row = functools.partial(_mod_row, tile=tq, n_ctx_tok=n_ctx_tok, lat_len=lat_len)
    aux = jax.ShapeDtypeStruct((PEER_HEADS, PEER_N_KEYS, t), F32)
    aux_spec = pl.BlockSpec((PEER_HEADS, PEER_N_KEYS, tq), lambda i: (0, 0, i))
    return pl.pallas_call(
        _peer_query_kernel,
        out_shape=(jax.ShapeDtypeStruct((t, d), BF16), aux, aux, aux, aux),
        grid=(t // tq,),
        in_specs=[pl.BlockSpec((tq, d), lambda i: (i, 0)),
                  pl.BlockSpec((1, d), lambda i: (0, 0)),
                  pl.BlockSpec((1, N_MOD, d), lambda i: (row(i), 0, 0)),
                  pl.BlockSpec(wq.shape, lambda i: (0, 0)),
                  pl.BlockSpec(keys_pad.shape, lambda i: (0, 0, 0))],
        out_specs=(pl.BlockSpec((tq, d), lambda i: (i, 0)), aux_spec, aux_spec, aux_spec, aux_spec),
        compiler_params=_cparams(("arbitrary",)),
        name="peer_query",
    )(x, g.reshape(1, d), mod, wq, keys_pad)


def _peer_main_kernel(x_ref, u_ref, vt_ref, s2_ref, e2_ref, thr_ref, e1_ref, o_ref, *, n_sub):
    k = pl.program_id(1)

    @pl.when(k == 0)
    def _():
        o_ref[...] = jnp.zeros_like(o_ref)

    nt = (((1,), (1,)), ((), ()))
    hid = _gelu_tanh(lax.dot_general(u_ref[...], x_ref[...], nt, preferred_element_type=F32))
    tt = hid.shape[1]
    parts = []
    for r in range(n_sub):
        i1 = k * n_sub + r
        w = jnp.zeros((PEER_N_KEYS, tt), F32)
        for hd in range(PEER_HEADS):
            thr = thr_ref[hd, pl.ds(i1, 1), :]
            e1 = e1_ref[hd, pl.ds(i1, 1), :]
            w = w + jnp.where(s2_ref[hd] >= thr, e2_ref[hd], 0.0) * e1
        parts.append((hid[r * PEER_N_KEYS:(r + 1) * PEER_N_KEYS] * w).astype(BF16))
    hw = jnp.concatenate(parts, axis=0) if n_sub > 1 else parts[0]
    o_ref[...] += jnp.dot(vt_ref[...], hw, preferred_element_type=F32)


def peer_main(h2, u, vt, s2, e2, thr, e1, tt=512, te=256):
    t, d = h2.shape
    n_exp = u.shape[0]
    aux_spec = pl.BlockSpec((PEER_HEADS, PEER_N_KEYS, tt), lambda i, k: (0, 0, i))
    return pl.pallas_call(
        functools.partial(_peer_main_kernel, n_sub=te // PEER_N_KEYS),
        out_shape=jax.ShapeDtypeStruct((d, t), F32),
        grid=(t // tt, n_exp // te),
        in_specs=[pl.BlockSpec((tt, d), lambda i, k: (i, 0)),
                  pl.BlockSpec((te, d), lambda i, k: (k, 0)),
                  pl.BlockSpec((d, te), lambda i, k: (0, k)),
                  aux_spec, aux_spec, aux_spec, aux_spec],
        out_specs=pl.BlockSpec((d, tt), lambda i, k: (0, i)),
        compiler_params=_cparams(("arbitrary", "arbitrary")),
        name="peer_main",
    )(h2, u, vt, s2, e2, thr, e1)


def _peer_residual_kernel(x_ref, pt_ref, mod_ref, g_ref, o_ref, *, final):
    x = x_ref[...] + mod_ref[0][5:6] * pt_ref[...].T
    if final:
        r = lax.rsqrt(jnp.mean(x * x, axis=-1, keepdims=True) + EPS)
        x = x * r * g_ref[...]
    o_ref[...] = x


def peer_residual(x, peer_t, mod, g_final, n_ctx_tok, lat_len, final, tr=256):
    t, d = x.shape
    row = functools.partial(_mod_row, tile=tr, n_ctx_tok=n_ctx_tok, lat_len=lat_len)
    return pl.pallas_call(
        functools.partial(_peer_residual_kernel, final=final),
        out_shape=jax.ShapeDtypeStruct((t, d), F32),
        grid=(t // tr,),
        in_specs=[pl.BlockSpec((tr, d), lambda i: (i, 0)),
                  pl.BlockSpec((d, tr), lambda i: (0, i)),
                  pl.BlockSpec((1, N_MOD, d), lambda i: (row(i), 0, 0)),
                  pl.BlockSpec((1, d), lambda i: (0, 0))],
        out_specs=pl.BlockSpec((tr, d), lambda i: (i, 0)),
        compiler_params=_cparams(("arbitrary",)),
        name="peer_residual",
    )(x, peer_t, mod, g_final.reshape(1, d))


_NT = (((1,), (1,)), ((), ()))


def _head_rmsnorm(x, g):
    return x * lax.rsqrt(jnp.mean(x * x, axis=-1, keepdims=True) + EPS) * g


def _rope_half(x, cosf, sinf):
    return x * cosf + pltpu.roll(x, GQA_HEAD_DIM // 2, 1) * sinf


def _rope_quarter(x, cosf, sinf):
    lane = lax.broadcasted_iota(jnp.int32, x.shape, 1)
    half = DIFF_QK_DIM // 2
    swapped = jnp.where((lane % DIFF_QK_DIM) < half,
                        pltpu.roll(x, LANES - half, 1), pltpu.roll(x, half, 1))
    return x * cosf + swapped * sinf


def _key_prep_kernel(gk_ref, fk_ref, kn_ref, cg_ref, sg_ref, cd_ref, sd_ref, kg_ref, kd_ref):
    for hh in range(GQA_KV_HEADS):
        sl = slice(hh * GQA_HEAD_DIM, (hh + 1) * GQA_HEAD_DIM)
        kg_ref[:, sl] = _rope_half(_head_rmsnorm(gk_ref[:, sl], kn_ref[...]), cg_ref[...], sg_ref[...])
    for hh in range(DIFF_HEADS):
        sl = slice(hh * LANES, (hh + 1) * LANES)
        kd_ref[:, sl] = _rope_quarter(fk_ref[:, sl], cd_ref[...], sd_ref[...])


def key_prep(z, k_norm, tabs, tm=256):
    t = z.shape[0]
    gk_blk = Z_START['gk'] // Z_WIDTH['gk']
    fk_blk = Z_START['fk'] // Z_WIDTH['fk']
    tab_spec = pl.BlockSpec((tm, LANES), lambda i: (i, 0))
    return pl.pallas_call(
        _key_prep_kernel,
        out_shape=(jax.ShapeDtypeStruct((t, Z_WIDTH['gk']), F32), jax.ShapeDtypeStruct((t, Z_WIDTH['fk']), F32)),
        grid=(t // tm,),
        in_specs=[pl.BlockSpec((tm, Z_WIDTH['gk']), lambda i: (i, gk_blk)),
                  pl.BlockSpec((tm, Z_WIDTH['fk']), lambda i: (i, fk_blk)),
                  pl.BlockSpec((1, GQA_HEAD_DIM), lambda i: (0, 0)),
                  tab_spec, tab_spec, tab_spec, tab_spec],
        out_specs=(pl.BlockSpec((tm, Z_WIDTH['gk']), lambda i: (i, 0)),
                   pl.BlockSpec((tm, Z_WIDTH['fk']), lambda i: (i, 0))),
        compiler_params=_cparams(("arbitrary",)),
        name="key_prep",
    )(z, z, k_norm.reshape(1, -1), *tabs)


def _softmax_rows(s):
    m = jnp.max(s, axis=-1, keepdims=True)
    p = jnp.exp(s - m)
    return p, jnp.sum(p, axis=-1, keepdims=True)


def _gqa_kernel(q_ref, qn_ref, cos_ref, sin_ref, k_ref, v_ref, o_ref):
    tq = q_ref.shape[0]
    n_rep = GQA_HEADS // GQA_KV_HEADS
    qs = []
    for hh in range(n_rep):
        x = _head_rmsnorm(q_ref[:, hh * GQA_HEAD_DIM:(hh + 1) * GQA_HEAD_DIM], qn_ref[...])
        x = _rope_half(x, cos_ref[...], sin_ref[...])
        qs.append((x * (GQA_HEAD_DIM ** -0.5)).astype(BF16))
    q = jnp.concatenate(qs, axis=0)
    s = lax.dot_general(q, k_ref[0], _NT, preferred_element_type=F32)
    p, l = _softmax_rows(s)
    o = jnp.dot(p.astype(BF16), v_ref[0], preferred_element_type=F32) / l
    o_ref[...] = jnp.concatenate([o[hh * tq:(hh + 1) * tq] for hh in range(n_rep)], axis=1).astype(o_ref.dtype)


def gqa_attention(z, q_norm, cos, sin, keys, vals, tok0, seq_len, tq):
    b, lk, _ = keys.shape
    n_rep = GQA_HEADS // GQA_KV_HEADS
    qw = n_rep * GQA_HEAD_DIM
    q_blk = Z_START['gq'] // qw
    nq = seq_len // tq
    row = lambda bi, g, qi: (tok0 + bi * seq_len) // tq + qi
    kv_spec = pl.BlockSpec((1, lk, GQA_HEAD_DIM), lambda bi, g, qi: (bi, 0, g))
    tab_spec = pl.BlockSpec((tq, LANES), lambda bi, g, qi: (row(bi, g, qi), 0))
    return pl.pallas_call(
        _gqa_kernel,
        out_shape=jax.ShapeDtypeStruct((b * seq_len, GROUP_W), BF16),
        grid=(b, GQA_KV_HEADS, nq),
        in_specs=[pl.BlockSpec((tq, qw), lambda bi, g, qi: (row(bi, g, qi), q_blk + g)),
                  pl.BlockSpec((1, GQA_HEAD_DIM), lambda bi, g, qi: (0, 0)),
                  tab_spec, tab_spec, kv_spec, kv_spec],
        out_specs=pl.BlockSpec((tq, qw), lambda bi, g, qi: (bi * nq + qi, g)),
        compiler_params=_cparams(("arbitrary", "arbitrary", "arbitrary")),
        name="gqa_attention",
    )(z, q_norm.reshape(1, -1), cos, sin, keys, vals)


def _diff_kernel(q_ref, cos_ref, sin_ref, k_ref, v_ref, lp_ref, g_ref, o_ref, *, lam_init):
    tq = q_ref.shape[0]
    lp = lp_ref[...]
    lam = (jnp.exp(jnp.sum(lp[0:1] * lp[1:2], axis=1, keepdims=True))
           - jnp.exp(jnp.sum(lp[2:3] * lp[3:4], axis=1, keepdims=True)) + lam_init)
    q = _rope_quarter(q_ref[...], cos_ref[...], sin_ref[...]) * (DIFF_QK_DIM ** -0.5)
    lane = lax.broadcasted_iota(jnp.int32, q.shape, 1)
    q12 = jnp.concatenate([jnp.where(lane < DIFF_QK_DIM, q, 0.0), jnp.where(lane >= DIFF_QK_DIM, q, 0.0)], axis=0)
    s = lax.dot_general(q12.astype(BF16), k_ref[0], _NT, preferred_element_type=F32)
    p, l = _softmax_rows(s)
    p = p / l
    w = p[:tq] - lam * p[tq:]
    o = jnp.dot(w.astype(BF16), v_ref[0], preferred_element_type=F32)
    o_ref[...] = (_head_rmsnorm(o, g_ref[...]) * (1.0 - lam_init)).astype(o_ref.dtype)


def diff_attention(z, cos, sin, keys, vals, lam_params, subln_g, layer_idx, tok0, seq_len, tq):
    b, lk, _ = keys.shape
    q_blk = Z_START['fq'] // LANES
    nq = seq_len // tq
    lam_init = 0.8 - 0.6 * math.exp(-0.3 * layer_idx)
    row = lambda bi, h, qi: (tok0 + bi * seq_len) // tq + qi
    kv_spec = pl.BlockSpec((1, lk, LANES), lambda bi, h, qi: (bi, 0, h))
    tab_spec = pl.BlockSpec((tq, LANES), lambda bi, h, qi: (row(bi, h, qi), 0))
    return pl.pallas_call(
        functools.partial(_diff_kernel, lam_init=lam_init),
        out_shape=jax.ShapeDtypeStruct((b * seq_len, GROUP_W), BF16),
        grid=(b, DIFF_HEADS, nq),
        in_specs=[pl.BlockSpec((tq, LANES), lambda bi, h, qi: (row(bi, h, qi), q_blk + h)),
                  tab_spec, tab_spec, kv_spec, kv_spec,
                  pl.BlockSpec(lam_params.shape, lambda bi, h, qi: (0, 0)),
                  pl.BlockSpec((1, DIFF_V_DIM), lambda bi, h, qi: (0, 0))],
        out_specs=pl.BlockSpec((tq, LANES), lambda bi, h, qi: (bi * nq + qi, h)),
        compiler_params=_cparams(("arbitrary", "arbitrary", "arbitrary")),
        name="diff_attention",
    )(z, cos, sin, keys, vals, lam_params, subln_g.reshape(1, -1))


def _rope_lane_tables(length, dim, n_ctx_tok, dec_batch):
    cos, sin = _rope_tables(length, dim)
    reps = LANES // dim
    cosf = jnp.tile(jnp.concatenate([cos, cos], axis=1), (dec_batch, reps))
    sinf = jnp.tile(jnp.concatenate([-sin, sin], axis=1), (dec_batch, reps))
    ones = jnp.ones((n_ctx_tok, LANES), F32)
    return jnp.concatenate([ones, cosf], axis=0), jnp.concatenate([jnp.zeros_like(ones), sinf], axis=0)


S5_BLK_GROUPS = LANES // S5_CH_PER_GROUP
S5_BLOCKS = S5_GROUPS // S5_BLK_GROUPS
S5_BLK_STATE = S5_BLK_GROUPS * S5_STATE
SUBLANES = 8


def _s5_tables(lam_re, lam_im, log_dt, b_re, b_im, c_re, c_im):
    dt = jnp.exp(log_dt)[..., None]
    mag = jnp.exp(lam_re * dt)
    ab_re, ab_im = mag * jnp.cos(lam_im * dt), mag * jnp.sin(lam_im * dt)
    den = lam_re * lam_re + lam_im * lam_im
    nr = ab_re - 1.0
    coef_re = (nr * lam_re + ab_im * lam_im) / den
    coef_im = (ab_im * lam_re - nr * lam_im) / den
    bb_re = coef_re[..., None] * b_re - coef_im[..., None] * b_im
    bb_im = coef_re[..., None] * b_im + coef_im[..., None] * b_re
    j, gl, p, h = S5_BLOCKS, S5_BLK_GROUPS, S5_STATE, S5_CH_PER_GROUP
    eye = jnp.eye(gl, dtype=F32)

    def in_blocks(bb):
        t = bb.reshape(2, j, gl, p, h)
        return jnp.einsum('djgph,gk->djghkp', t, eye).reshape(2, j, gl * h, gl * p)

    def out_blocks(cc):
        t = cc.reshape(2, j, gl, h, p)
        return jnp.einsum('djghp,gk->djgpkh', t, eye).reshape(2, j, gl * p, gl * h)

    wb = jnp.concatenate([in_blocks(bb_re), in_blocks(bb_im)], axis=-1).astype(BF16)
    wc = jnp.concatenate([out_blocks(c_re), out_blocks(-c_im)], axis=-2).astype(BF16)

    ar, ai = ab_re.reshape(2, j, 1, gl * p), ab_im.reshape(2, j, 1, gl * p)
    pows = [(ar, ai)]
    for _ in range(SUBLANES - 1):
        pr, pi = pows[-1]
        pows.append((pr * ar - pi * ai, pr * ai + pi * ar))
    r = jnp.arange(SUBLANES).reshape(1, 1, SUBLANES, 1)
    rows = []
    for d in range(2):
        dr = []
        for s in (1, 2, 4):
            mask = (r >= s) if d == 0 else (r <= SUBLANES - 1 - s)
            dr += [jnp.where(mask, pows[s - 1][0][d:d + 1], 0.0), jnp.where(mask, pows[s - 1][1][d:d + 1], 0.0)]
        order = range(SUBLANES) if d == 0 else range(SUBLANES - 1, -1, -1)
        dr.append(jnp.concatenate([pows[k][0][d:d + 1] for k in order], axis=2))
        dr.append(jnp.concatenate([pows[k][1][d:d + 1] for k in order], axis=2))
        rows.append(jnp.stack([jnp.broadcast_to(x, (1, j, SUBLANES, gl * p)) for x in dr], axis=2))
    return wb, wc, jnp.concatenate(rows, axis=0)


def _s5_scan_tile(xr, xi, coef_ref, d, hr, hi):
    for k, s in enumerate((1, 2, 4)):
        shift = s if d == 0 else SUBLANES - s
        ar, ai = coef_ref[d, 0, 2 * k], coef_ref[d, 0, 2 * k + 1]
        sr, si = pltpu.roll(xr, shift, 0), pltpu.roll(xi, shift, 0)
        xr, xi = xr + ar * sr - ai * si, xi + ar * si + ai * sr
    cr, ci = coef_ref[d, 0, 6], coef_ref[d, 0, 7]
    return xr + cr * hr - ci * hi, xi + cr * hi + ci * hr


def _s5_scan_kernel(uf_ref, ub_ref, wb_ref, wc_ref, coef_ref, h0_ref, yf_ref, yb_ref, hfin_ref,
                    sr_ref, si_ref, carry_ref):
    c = pl.program_id(2)
    bs = S5_BLK_STATE
    tc = uf_ref.shape[0]
    n_tiles = tc // SUBLANES

    @pl.when(c == 0)
    def _():
        carry_ref[...] = h0_ref[0, :, 0]

    for d, (u_ref, y_ref) in enumerate(((uf_ref, yf_ref), (ub_ref, yb_ref))):
        bu = jnp.dot(u_ref[...].astype(BF16), wb_ref[d, 0], preferred_element_type=F32)
        sr_ref[...] = bu[:, :bs]
        si_ref[...] = bu[:, bs:]

        def body(i, carry, d=d):
            hr, hi = carry
            tile = i if d == 0 else n_tiles - 1 - i
            rows = pl.ds(pl.multiple_of(tile * SUBLANES, SUBLANES), SUBLANES)
            xr, xi = _s5_scan_tile(sr_ref[rows, :], si_ref[rows, :], coef_ref, d, hr, hi)
            sr_ref[rows, :] = xr
            si_ref[rows, :] = xi
            last = SUBLANES - 1 if d == 0 else 0
            return xr[last:last + 1], xi[last:last + 1]

        h0 = carry_ref[d]
        hr, hi = lax.fori_loop(0, n_tiles, body, (h0[:, :bs], h0[:, bs:]))
        carry_ref[d] = jnp.concatenate([hr, hi], axis=1)
        hcat = jnp.concatenate([sr_ref[...], si_ref[...]], axis=1).astype(BF16)
        y_ref[...] = jnp.dot(hcat, wc_ref[d, 0], preferred_element_type=F32)

    hfin_ref[0, :, 0] = carry_ref[...]


def s5_scan(z, wb, wc, coef, h0, tok0, seq_len, tc):
    b = h0.shape[0]
    nt = seq_len // tc
    width = 2 * S5_BLK_STATE
    u_blk = Z_START['s5_u'] // LANES
    row_f = lambda bi, j, c: ((tok0 + bi * seq_len) // tc + c, u_blk + j)
    row_b = lambda bi, j, c: ((tok0 + bi * seq_len) // tc + nt - 1 - c, u_blk + j)
    st_spec = pl.BlockSpec((1, 2, 1, 1, width), lambda bi, j, c: (bi, 0, j, 0, 0))
    y_shape = jax.ShapeDtypeStruct((b * seq_len, S5_WIDTH), F32)
    return pl.pallas_call(
        _s5_scan_kernel,
        out_shape=(y_shape, y_shape, jax.ShapeDtypeStruct(h0.shape, F32)),
        grid=(b, S5_BLOCKS, nt),
        in_specs=[pl.BlockSpec((tc, LANES), row_f),
                  pl.BlockSpec((tc, LANES), row_b),
                  pl.BlockSpec((2, 1, LANES, width), lambda bi, j, c: (0, j, 0, 0)),
                  pl.BlockSpec((2, 1, width, LANES), lambda bi, j, c: (0, j, 0, 0)),
                  pl.BlockSpec((2, 1, SUBLANES, SUBLANES, S5_BLK_STATE), lambda bi, j, c: (0, j, 0, 0, 0)),
                  st_spec],
        out_specs=(pl.BlockSpec((tc, LANES), lambda bi, j, c: (bi * nt + c, j)),
                   pl.BlockSpec((tc, LANES), lambda bi, j, c: (bi * nt + nt - 1 - c, j)),
                   st_spec),
        scratch_shapes=[pltpu.VMEM((tc, S5_BLK_STATE), F32), pltpu.VMEM((tc, S5_BLK_STATE), F32),
                        pltpu.VMEM((2, 1, width), F32)],
        compiler_params=_cparams(("arbitrary", "arbitrary", "arbitrary")),
        name="s5_scan",
    )(z, z, wb, wc, coef, h0)


def _s5_glu_kernel(yf_ref, yb_ref, u_ref, d_ref, w_ref, o_ref):
    y = _gelu_tanh(yf_ref[...] + yb_ref[...] + d_ref[...] * u_ref[...])
    gate = jnp.dot(y.astype(BF16), w_ref[...], preferred_element_type=F32)
    o_ref[...] = (y * (1.0 / (1.0 + jnp.exp(-gate)))).astype(o_ref.dtype)


def s5_glu(yf, yb, z, d_skip, w_glu, tm=512):
    t = z.shape[0]
    blk = pl.BlockSpec((tm, S5_WIDTH), lambda i: (i, 0))
    return pl.pallas_call(
        _s5_glu_kernel,
        out_shape=jax.ShapeDtypeStruct((t, S5_WIDTH), BF16),
        grid=(t // tm,),
        in_specs=[blk, blk, pl.BlockSpec((tm, S5_WIDTH), lambda i: (i, Z_START['s5_u'] // S5_WIDTH)),
                  pl.BlockSpec((1, S5_WIDTH), lambda i: (0, 0)),
                  pl.BlockSpec((S5_WIDTH, S5_WIDTH), lambda i: (0, 0))],
        out_specs=blk,
        compiler_params=_cparams(("arbitrary",)),
        name="s5_glu",
    )(yf, yb, z, d_skip.reshape(1, -1), w_glu)


def _s5_state_in(re, im):
    b = re.shape[0]
    r = re.reshape(b, 2, S5_BLOCKS, 1, S5_BLK_STATE)
    i = im.reshape(b, 2, S5_BLOCKS, 1, S5_BLK_STATE)
    return jnp.concatenate([r, i], axis=-1)


def _s5_state_out(h):
    b = h.shape[0]
    re = h[..., :S5_BLK_STATE].reshape(b, 2, S5_GROUPS, S5_STATE)
    im = h[..., S5_BLK_STATE:].reshape(b, 2, S5_GROUPS, S5_STATE)
    return re, im


def _dn_conv_kernel(x_ref, prev_ref, next_ref, w_ref, o_ref, *, n_ctx_tok, seq, lat_len):
    i = pl.program_id(0)
    cg = pl.program_id(1)
    tm = x_ref.shape[0]
    t0 = i * tm
    in_ctx = t0 < n_ctx_tok
    pos = jnp.where(in_ctx, t0 % seq, (t0 - n_ctx_tok) % lat_len)
    length = jnp.where(in_ctx, seq, lat_len)
    x = x_ref[...]
    row = lax.broadcasted_iota(jnp.int32, x.shape, 0)
    before = jnp.where(pos > 0, prev_ref[SUBLANES - 1:SUBLANES, :], 0.0)
    after = jnp.where(pos + tm < length, next_ref[0:1, :], 0.0)
    xp = jnp.where(row == 0, before, pltpu.roll(x, 1, 0))
    xn = jnp.where(row == tm - 1, after, pltpu.roll(x, tm - 1, 0))
    y = _silu(w_ref[0:1, :] * xp + w_ref[1:2, :] * x + w_ref[2:3, :] * xn)
    scale = jnp.where(cg == 0, DN_HEAD_DIM ** -0.5, 1.0)
    for hh in range(DN_HEADS):
        sl = slice(hh * DN_HEAD_DIM, (hh + 1) * DN_HEAD_DIM)
        yh = y[:, sl]
        nrm = yh * (lax.rsqrt(jnp.sum(yh * yh, axis=-1, keepdims=True) + EPS) * scale)
        o_ref[:, sl] = jnp.where(cg == 2, yh, nrm)


def dn_conv(z, conv_w, n_ctx_tok, seq, lat_len, tm=256):
    t = z.shape[0]
    first = Z_START['dn_q'] // GROUP_W
    nsub = tm // SUBLANES
    last_sub = t // SUBLANES - 1
    return pl.pallas_call(
        functools.partial(_dn_conv_kernel, n_ctx_tok=n_ctx_tok, seq=seq, lat_len=lat_len),
        out_shape=jax.ShapeDtypeStruct((t, 3 * GROUP_W), F32),
        grid=(t // tm, 3),
        in_specs=[pl.BlockSpec((tm, GROUP_W), lambda i, cg: (i, first + cg)),
                  pl.BlockSpec((SUBLANES, GROUP_W), lambda i, cg: (jnp.maximum(i * nsub - 1, 0), first + cg)),
                  pl.BlockSpec((SUBLANES, GROUP_W), lambda i, cg: (jnp.minimum((i + 1) * nsub, last_sub), first + cg)),
                  pl.BlockSpec((3, GROUP_W), lambda i, cg: (0, cg))],
        out_specs=pl.BlockSpec((tm, GROUP_W), lambda i, cg: (i, cg)),
        compiler_params=_cparams(("arbitrary", "arbitrary")),
        name="dn_conv",
    )(z, z, z, conv_w)


def _dn_gate_kernel(x_ref, neg_a_ref, dtb_ref, o_ref):
    xt = x_ref[...].T
    nh = 2 * DN_HEADS
    a = xt[0:nh] + dtb_ref[...]
    softplus = jnp.maximum(a, 0.0) + jnp.log(1.0 + jnp.exp(-jnp.abs(a)))
    o_ref[0:nh, :] = neg_a_ref[...] * softplus
    o_ref[nh:2 * nh, :] = 1.0 / (1.0 + jnp.exp(-xt[nh:2 * nh]))


def dn_gates(z, a_log, dt_bias, tm=256):
    t = z.shape[0]
    nh = 2 * DN_HEADS
    neg_a = jnp.broadcast_to(-jnp.exp(a_log).reshape(nh, 1), (nh, tm))
    dtb = jnp.broadcast_to(dt_bias.reshape(nh, 1), (nh, tm))
    return pl.pallas_call(
        _dn_gate_kernel,
        out_shape=jax.ShapeDtypeStruct((2 * nh, t), F32),
        grid=(t // tm,),
        in_specs=[pl.BlockSpec((tm, LANES), lambda i: (i, Z_START['dn_a'] // LANES)),
                  pl.BlockSpec((nh, tm), lambda i: (0, 0)),
                  pl.BlockSpec((nh, tm), lambda i: (0, 0))],
        out_specs=pl.BlockSpec((2 * nh, tm), lambda i: (0, i)),
        compiler_params=_cparams(("arbitrary",)),
        name="dn_gates",
    )(z, neg_a, dtb)


def _unit_tri_inverse(a, rows, cols, upper):
    n = a.shape[0]
    eye = (rows == cols).astype(F32)
    s = 1
    t = None
    while s < n:
        same = (rows ^ cols) < 2 * s
        lo, hi = (rows & s) != 0, (cols & s) == 0
        if upper:
            lo, hi = (rows & s) == 0, (cols & s) != 0
        off = jnp.where(same, jnp.where(lo, jnp.where(hi, a, 0.0), 0.0), 0.0)
        if t is None:
            t = eye - off
        else:
            tb = t.astype(BF16)
            mid = jnp.dot(off.astype(BF16), tb, preferred_element_type=F32)
            t = t - jnp.dot(tb, mid.astype(BF16), preferred_element_type=F32)
        s *= 2
    return t


def _dn_chunk_step(q, k, v, g_row, beta_row, s_prev, upper):
    c = q.shape[0]
    rows = lax.broadcasted_iota(jnp.int32, (c, c), 0)
    cols = lax.broadcasted_iota(jnp.int32, (c, c), 1)
    eye = rows == cols
    causal = (rows <= cols) if upper else (rows >= cols)
    strict = (rows < cols) if upper else (rows > cols)
    g_col = jnp.sum(jnp.where(eye, g_row, 0.0), axis=1, keepdims=True)
    beta_col = jnp.sum(jnp.where(eye, beta_row, 0.0), axis=1, keepdims=True)
    causal_t = (rows >= cols) if upper else (rows <= cols)
    gc_col = jnp.sum(jnp.where(causal, g_row, 0.0), axis=1, keepdims=True)
    gc_row = jnp.sum(jnp.where(causal_t, g_col, 0.0), axis=0, keepdims=True)
    g_last = jnp.sum(g_row, axis=1, keepdims=True)
    decay = jnp.where(causal, jnp.exp(jnp.where(causal, gc_col - gc_row, 0.0)), 0.0)
    kb = k * beta_col
    kbf = k.astype(BF16)
    a = jnp.where(strict, lax.dot_general(kb.astype(BF16), kbf, _NT, preferred_element_type=F32) * decay, 0.0)
    tinv = _unit_tri_inverse(a, rows, cols, upper)
    rhs = jnp.concatenate([v * beta_col, kb * jnp.exp(gc_col)], axis=1)
    sol = jnp.dot(tinv.astype(BF16), rhs.astype(BF16), preferred_element_type=F32)
    dv = v.shape[1]
    u, w = sol[:, :dv], sol[:, dv:]
    sb = s_prev.astype(BF16)
    v_new = u - jnp.dot(w.astype(BF16), sb, preferred_element_type=F32)
    attn = lax.dot_general(q.astype(BF16), kbf, _NT, preferred_element_type=F32) * decay
    o = (jnp.dot((q * jnp.exp(gc_col)).astype(BF16), sb, preferred_element_type=F32)
         + jnp.dot(attn.astype(BF16), v_new.astype(BF16), preferred_element_type=F32))
    k_dec = (k * jnp.exp(g_last - gc_col)).T
    s_new = s_prev * jnp.exp(g_last) + jnp.dot(k_dec.astype(BF16), v_new.astype(BF16), preferred_element_type=F32)
    return o, s_new


def _dn_chunk_kernel(qf_ref, kf_ref, vf_ref, gf_ref, qb_ref, kb_ref, vb_ref, gb_ref, s0_ref,
                     of_ref, ob_ref, sfin_ref, s_ref):
    hd = pl.program_id(1)
    c = pl.program_id(2)

    @pl.when(c == 0)
    def _():
        s_ref[...] = s0_ref[0, :, 0]

    n_sub = qf_ref.shape[0] // DN_CHUNK
    nh = 2 * DN_HEADS
    for d, (q_ref, k_ref, v_ref, g_ref, o_ref) in enumerate(((qf_ref, kf_ref, vf_ref, gf_ref, of_ref),
                                                              (qb_ref, kb_ref, vb_ref, gb_ref, ob_ref))):
        g_all = g_ref[pl.ds(d * DN_HEADS + hd, 1), :]
        beta_all = g_ref[pl.ds(nh + d * DN_HEADS + hd, 1), :]
        s = s_ref[d]
        subs = range(n_sub) if d == 0 else range(n_sub - 1, -1, -1)
        for j in subs:
            tok = slice(j * DN_CHUNK, (j + 1) * DN_CHUNK)
            o, s = _dn_chunk_step(q_ref[tok, :], k_ref[tok, :], v_ref[tok, :],
                                  g_all[:, tok], beta_all[:, tok], s, upper=(d == 1))
            o_ref[tok, :] = o
        s_ref[d] = s
    sfin_ref[0, :, 0] = s_ref[...]


def dn_chunk(qkv, gates, s0, tok0, seq_len, blk=128):
    b = s0.shape[0]
    nb = seq_len // blk
    hdim = DN_HEAD_DIM
    fwd = lambda bi, c: (tok0 + bi * seq_len) // blk + c
    bwd = lambda bi, c: (tok0 + bi * seq_len) // blk + nb - 1 - c

    def col_spec(pos, o):
        return pl.BlockSpec((blk, hdim), lambda bi, h, c: (pos(bi, c), o * DN_HEADS + h))

    def specs(pos):
        return [col_spec(pos, o) for o in range(3)] + [
            pl.BlockSpec((4 * DN_HEADS, blk), lambda bi, h, c: (0, pos(bi, c)))]

    st_spec = pl.BlockSpec((1, 2, 1, hdim, hdim), lambda bi, h, c: (bi, 0, h, 0, 0))
    o_shape = jax.ShapeDtypeStruct((b * seq_len, GROUP_W), F32)
    return pl.pallas_call(
        _dn_chunk_kernel,
        out_shape=(o_shape, o_shape, jax.ShapeDtypeStruct(s0.shape, F32)),
        grid=(b, DN_HEADS, nb),
        in_specs=specs(fwd) + specs(bwd) + [st_spec],
        out_specs=(pl.BlockSpec((blk, hdim), lambda bi, h, c: (bi * nb + c, h)),
                   pl.BlockSpec((blk, hdim), lambda bi, h, c: (bi * nb + nb - 1 - c, h)),
                   st_spec),
        scratch_shapes=[pltpu.VMEM((2, hdim, hdim), F32)],
        compiler_params=_cparams(("arbitrary", "arbitrary", "arbitrary")),
        name="dn_chunk",
    )(qkv, qkv, qkv, gates, qkv, qkv, qkv, gates, s0)


def _dn_post_kernel(of_ref, ob_ref, gate_ref, g_ref, o_ref):
    for hh in range(DN_HEADS):
        sl = slice(hh * DN_HEAD_DIM, (hh + 1) * DN_HEAD_DIM)
        o = _head_rmsnorm(of_ref[:, sl] + ob_ref[:, sl], g_ref[...])
        o_ref[:, sl] = (o * _silu(gate_ref[:, sl])).astype(o_ref.dtype)


def dn_post(o_f, o_b, z, norm_g, tm=512):
    t = z.shape[0]
    blk = pl.BlockSpec((tm, GROUP_W), lambda i: (i, 0))
    return pl.pallas_call(
        _dn_post_kernel,
        out_shape=jax.ShapeDtypeStruct((t, GROUP_W), BF16),
        grid=(t // tm,),
        in_specs=[blk, blk, pl.BlockSpec((tm, GROUP_W), lambda i: (i, Z_START['dn_gate'] // GROUP_W)),
                  pl.BlockSpec((1, DN_HEAD_DIM), lambda i: (0, 0))],
        out_specs=blk,
        compiler_params=_cparams(("arbitrary",)),
        name="dn_post",
    )(o_f, o_b, z, norm_g.reshape(1, -1))


def _rope_tables(length, dim):
    n_rows = length // GRID_W
    row = jnp.repeat(jnp.arange(n_rows), GRID_W).astype(F32)
    col = jnp.tile(jnp.arange(GRID_W), n_rows).astype(F32)
    quarter = dim // 4
    freqs = ROPE_THETA ** (-jnp.arange(quarter, dtype=F32) / quarter)
    ang = jnp.concatenate([row[:, None] * freqs, col[:, None] * freqs], axis=-1)
    return jnp.cos(ang), jnp.sin(ang)


def _permute_w_in(w):
    cols = [w[:, Z_REF_START[n]:Z_REF_START[n] + Z_WIDTH[n]] for n in _Z_ORDER]
    used = sum(Z_WIDTH[n] for n in _Z_ORDER)
    cols.append(jnp.zeros((w.shape[0], Z_COLS - used), w.dtype))
    return jnp.concatenate(cols, axis=1).astype(BF16)


def _pad_keys(keys):
    h, two, n, half = keys.shape
    z = jnp.zeros((h, n, half), keys.dtype)
    k0 = jnp.concatenate([keys[:, 0], z], axis=-1)
    k1 = jnp.concatenate([z, keys[:, 1]], axis=-1)
    return jnp.stack([k0, k1], axis=1).reshape(2 * h, n, 2 * half).astype(BF16)


def kernel(x_prompt, x_sample, c, cache_gqa_k, cache_gqa_v, cache_diff_k, cache_diff_v, state_s5_re, state_s5_im, state_delta, c_ctx, w_mod, b_mod, norm1_g, norm2_g, w_in, s5_lambda_re, s5_lambda_im, s5_log_dt, s5_b_re, s5_b_im, s5_c_re, s5_c_im, s5_d, s5_w_glu, dn_conv_w, dn_a_log, dn_dt_bias, dn_norm_g, gqa_q_norm, gqa_k_norm, diff_lambda, diff_subln_g, w_out, peer_w_q, peer_keys, peer_u, peer_v, final_norm_g):
    batch, seq, d = x_prompt.shape
    dec_batch, lat_len, _ = x_sample.shape
    depth = w_in.shape[0]
    n_ctx_tok = batch * seq
    n_lat_tok = dec_batch * lat_len
    tab_g = _rope_lane_tables(lat_len, GQA_HEAD_DIM, n_ctx_tok, dec_batch)
    tab_d = _rope_lane_tables(lat_len, DIFF_QK_DIM, n_ctx_tok, dec_batch)
    kv_w = GQA_KV_HEADS * GQA_HEAD_DIM

    def zcol(z, name, lo, hi):
        return z[lo:hi, Z_START[name]:Z_START[name] + Z_WIDTH[name]]

    def with_cache(cache, new, width):
        return jnp.concatenate([cache.reshape(dec_batch, -1, width), new.reshape(dec_batch, lat_len, width)],
                               axis=1).astype(BF16)

    cond = jnp.concatenate([c_ctx[None], c, jnp.zeros((8 - 1 - dec_batch, d), F32)], axis=0)
    mod_all = modulation(cond, w_mod, b_mod).reshape(depth, 8, N_MOD, d)

    x = jnp.concatenate([x_prompt.reshape(n_ctx_tok, d), x_sample.reshape(dec_batch * lat_len, d)], axis=0)
    new_state = [[] for _ in range(7)]
    for l in range(depth):
        mod = mod_all[l]
        z = in_projection(x, norm1_g[l], mod, _permute_w_in(w_in[l]), n_ctx_tok, lat_len)

        wb, wc, coef = _s5_tables(s5_lambda_re[l], s5_lambda_im[l], s5_log_dt[l],
                                  s5_b_re[l], s5_b_im[l], s5_c_re[l], s5_c_im[l])
        h0_ctx = jnp.zeros((batch, 2, S5_BLOCKS, 1, 2 * S5_BLK_STATE), F32)
        yf_c, yb_c, s5_fin = s5_scan(z, wb, wc, coef, h0_ctx, 0, seq, tc=seq)
        yf_l, yb_l, _ = s5_scan(z, wb, wc, coef, _s5_state_in(state_s5_re[:, l], state_s5_im[:, l]),
                                n_ctx_tok, lat_len, tc=256)
        y_a = s5_glu(jnp.concatenate([yf_c, yf_l], axis=0), jnp.concatenate([yb_c, yb_l], axis=0),
                     z, s5_d[l], s5_w_glu[l].astype(BF16))
        s5_re, s5_im = _s5_state_out(s5_fin)

        qkv = dn_conv(z, dn_conv_w[l], n_ctx_tok, seq, lat_len)
        gates = dn_gates(z, dn_a_log[l], dn_dt_bias[l])
        s0_ctx = jnp.zeros((batch, 2, DN_HEADS, DN_HEAD_DIM, DN_HEAD_DIM), F32)
        of_c, ob_c, delta = dn_chunk(qkv, gates, s0_ctx, 0, seq)
        of_l, ob_l, _ = dn_chunk(qkv, gates, state_delta[:, l], n_ctx_tok, lat_len)
        y_b = dn_post(jnp.concatenate([of_c, of_l], axis=0), jnp.concatenate([ob_c, ob_l], axis=0),
                      z, dn_norm_g[l])

        kg, kd = key_prep(z, gqa_k_norm[l], (*tab_g, *tab_d))
        gk_ctx, gv_ctx = kg[:n_ctx_tok], zcol(z, 'gv', 0, n_ctx_tok)
        fk_ctx, fv_ctx = kd[:n_ctx_tok], zcol(z, 'fv', 0, n_ctx_tok)
        yc_c = gqa_attention(z, gqa_q_norm[l], *tab_g, gk_ctx.reshape(batch, seq, kv_w).astype(BF16),
                             gv_ctx.reshape(batch, seq, kv_w).astype(BF16), 0, seq, tq=seq)
        yc_l = gqa_attention(z, gqa_q_norm[l], *tab_g, with_cache(cache_gqa_k[:, l], kg[n_ctx_tok:], kv_w),
                             with_cache(cache_gqa_v[:, l], zcol(z, 'gv', n_ctx_tok, None), kv_w),
                             n_ctx_tok, lat_len, tq=128)
        yd_c = diff_attention(z, *tab_d, fk_ctx.reshape(batch, seq, GROUP_W).astype(BF16),
                              fv_ctx.reshape(batch, seq, GROUP_W).astype(BF16),
                              diff_lambda[l], diff_subln_g[l], l, 0, seq, tq=seq)
        yd_l = diff_attention(z, *tab_d, with_cache(cache_diff_k[:, l], kd[n_ctx_tok:], GROUP_W),
                              with_cache(cache_diff_v[:, l], zcol(z, 'fv', n_ctx_tok, None), GROUP_W),
                              diff_lambda[l], diff_subln_g[l], l, n_ctx_tok, lat_len, tq=256)
        y_c = jnp.concatenate([yc_c, yc_l], axis=0)
        y_d = jnp.concatenate([yd_c, yd_l], axis=0)
        ctx_out = (gk_ctx.reshape(batch, seq, GQA_KV_HEADS, GQA_HEAD_DIM),
                   gv_ctx.reshape(batch, seq, GQA_KV_HEADS, GQA_HEAD_DIM),
                   fk_ctx.reshape(batch, seq, DIFF_HEADS, 2, DIFF_QK_DIM),
                   fv_ctx.reshape(batch, seq, DIFF_HEADS, DIFF_V_DIM),
                   s5_re, s5_im, delta)

        x = out_projection((y_a, y_b, y_c, y_d), w_out[l].astype(BF16), x, mod, n_ctx_tok, lat_len)
        h2, s2, e2, thr, e1 = peer_query(x, norm2_g[l], mod, peer_w_q[l].astype(BF16), _pad_keys(peer_keys[l]),
                                         n_ctx_tok, lat_len)
        peer_t = peer_main(h2, peer_u[l].astype(BF16), peer_v[l].astype(BF16).T, s2, e2, thr, e1)
        x = peer_residual(x, peer_t, mod, final_norm_g, n_ctx_tok, lat_len, final=(l == depth - 1))
        for acc, val in zip(new_state, ctx_out):
            acc.append(val)
    y_prompt = x[:n_ctx_tok].reshape(batch, seq, d)
    y_sample = x[n_ctx_tok:].reshape(dec_batch, lat_len, d)
    return (y_prompt, y_sample) + tuple(jnp.stack(s, axis=1) for s in new_state)
```

```python
import functools
import math

import jax
import jax.numpy as jnp
from jax import lax
from jax.experimental import pallas as pl
from jax.experimental.pallas import tpu as pltpu

F32 = jnp.float32
BF16 = jnp.bfloat16

D_MODEL = 4096
GRID_W = 64
GROUP_W = D_MODEL // 4
S5_WIDTH = GROUP_W
S5_CH_PER_GROUP = 16
S5_GROUPS = S5_WIDTH // S5_CH_PER_GROUP
S5_STATE = 64
DN_HEADS = 8
DN_HEAD_DIM = GROUP_W // DN_HEADS
DN_CHUNK = 64
GQA_HEADS = 8
GQA_KV_HEADS = 2
GQA_HEAD_DIM = GROUP_W // GQA_HEADS
DIFF_HEADS = 8
DIFF_V_DIM = GROUP_W // DIFF_HEADS
DIFF_QK_DIM = DIFF_V_DIM // 2
Q_BLOCK = 128
ROPE_THETA = 10000.0
PEER_HEADS = 8
PEER_KEY_DIM = 128
PEER_N_KEYS = 128
PEER_TOPK = 16
N_MOD = 6
EPS = 1e-6

_REF_SPLITS = (('s5_u', 1024), ('dn_q', 1024), ('dn_k', 1024), ('dn_v', 1024), ('dn_gate', 1024),
               ('dn_a', 16), ('dn_b', 16), ('gq', 1024), ('gk', 256), ('gv', 256),
               ('fq', 1024), ('fk', 1024), ('fv', 1024))
_Z_ORDER = ('s5_u', 'dn_q', 'dn_k', 'dn_v', 'dn_gate', 'gq', 'fq', 'fk', 'fv', 'gk', 'gv', 'dn_a', 'dn_b')
LANES = 128
VMEM_LIMIT = 56 * 1024 * 1024


def _z_layout():
    widths = dict(_REF_SPLITS)
    ref_start, s = {}, 0
    for name, w in _REF_SPLITS:
        ref_start[name] = s
        s += w
    z_start, s = {}, 0
    for name in _Z_ORDER:
        z_start[name] = s
        s += widths[name]
    total = -(-s // LANES) * LANES
    return widths, ref_start, z_start, total


Z_WIDTH, Z_REF_START, Z_START, Z_COLS = _z_layout()


def _cparams(sem):
    return pltpu.CompilerParams(dimension_semantics=sem, vmem_limit_bytes=VMEM_LIMIT)


def _silu(x):
    return x * (1.0 / (1.0 + jnp.exp(-x)))


def _gelu_tanh(x):
    return 0.5 * x * (1.0 + jnp.tanh(math.sqrt(2.0 / math.pi) * (x + 0.044715 * (x * x * x))))


def _mod_kernel(cond_ref, w_ref, b_ref, o_ref):
    a = _silu(cond_ref[...]).astype(BF16)
    o_ref[0] = jnp.dot(a, w_ref[0].astype(BF16), preferred_element_type=F32) + b_ref[0]


def modulation(cond, w_mod, b_mod, tn=512):
    depth, d, n = w_mod.shape
    return pl.pallas_call(
        _mod_kernel,
        out_shape=jax.ShapeDtypeStruct((depth, 8, n), F32),
        grid=(depth, n // tn),
        in_specs=[pl.BlockSpec((8, d), lambda l, j: (0, 0)),
                  pl.BlockSpec((1, d, tn), lambda l, j: (l, 0, j)),
                  pl.BlockSpec((1, 1, tn), lambda l, j: (l, 0, j))],
        out_specs=pl.BlockSpec((1, 8, tn), lambda l, j: (l, 0, j)),
        compiler_params=_cparams(("arbitrary", "arbitrary")),
        name="modulation",
    )(cond, w_mod, b_mod.reshape(depth, 1, n))


def _mod_row(i, tile, n_ctx_tok, lat_len):
    t0 = i * tile
    return jnp.where(t0 < n_ctx_tok, 0, 1 + (t0 - n_ctx_tok) // lat_len)


def _norm_mod(x, g, shift, scale):
    r = lax.rsqrt(jnp.mean(x * x, axis=-1, keepdims=True) + EPS)
    return (x * r * g) * (1.0 + scale) + shift


def _inproj_kernel(x_ref, g_ref, mod_ref, w_ref, o_ref, h_ref):
    @pl.when(pl.program_id(1) == 0)
    def _():
        m = mod_ref[0]
        h_ref[...] = _norm_mod(x_ref[...], g_ref[...], m[0:1], m[1:2]).astype(BF16)

    o_ref[...] = jnp.dot(h_ref[...], w_ref[...], preferred_element_type=F32).astype(o_ref.dtype)


def in_projection(x, g, mod, w, n_ctx_tok, lat_len, tm=512, tn=896):
    t, d = x.shape
    n = w.shape[1]
    row = functools.partial(_mod_row, tile=tm, n_ctx_tok=n_ctx_tok, lat_len=lat_len)
    return pl.pallas_call(
        _inproj_kernel,
        out_shape=jax.ShapeDtypeStruct((t, n), F32),
        grid=(t // tm, n // tn),
        in_specs=[pl.BlockSpec((tm, d), lambda i, j: (i, 0)),
                  pl.BlockSpec((1, d), lambda i, j: (0, 0)),
                  pl.BlockSpec((1, N_MOD, d), lambda i, j: (row(i), 0, 0)),
                  pl.BlockSpec((d, tn), lambda i, j: (0, j))],
        out_specs=pl.BlockSpec((tm, tn), lambda i, j: (i, j)),
        scratch_shapes=[pltpu.VMEM((tm, d), BF16)],
        compiler_params=_cparams(("arbitrary", "arbitrary")),
        name="in_projection",
    )(x, g.reshape(1, d), mod, w)


def _outproj_kernel(ya_ref, yb_ref, yc_ref, yd_ref, w_ref, x_ref, mod_ref, o_ref):
    acc = None
    for m, y_ref in enumerate((ya_ref, yb_ref, yc_ref, yd_ref)):
        part = jnp.dot(y_ref[...], w_ref[m * GROUP_W:(m + 1) * GROUP_W, :], preferred_element_type=F32)
        acc = part if acc is None else acc + part
    o_ref[...] = x_ref[...] + mod_ref[0][2:3] * acc


def out_projection(ys, w, x, mod, n_ctx_tok, lat_len, tm=512, tn=1024):
    t, n = x.shape
    kdim = w.shape[0]
    row = functools.partial(_mod_row, tile=tm, n_ctx_tok=n_ctx_tok, lat_len=lat_len)
    y_spec = pl.BlockSpec((tm, GROUP_W), lambda i, j: (i, 0))
    return pl.pallas_call(
        _outproj_kernel,
        out_shape=jax.ShapeDtypeStruct((t, n), F32),
        grid=(t // tm, n // tn),
        in_specs=[y_spec, y_spec, y_spec, y_spec,
                  pl.BlockSpec((kdim, tn), lambda i, j: (0, j)),
                  pl.BlockSpec((tm, tn), lambda i, j: (i, j)),
                  pl.BlockSpec((1, N_MOD, tn), lambda i, j: (row(i), 0, j))],
        out_specs=pl.BlockSpec((tm, tn), lambda i, j: (i, j)),
        compiler_params=_cparams(("arbitrary", "arbitrary")),
        name="out_projection",
    )(*ys, w, x, mod)


def _top_rows(x, k):
    vals = []
    cur = x
    for _ in range(k):
        mx = jnp.max(cur, axis=0, keepdims=True)
        vals.append(mx)
        cur = jnp.where(cur == mx, -jnp.inf, cur)
    return jnp.concatenate(vals, axis=0)


def _pair_sums(a, b):
    k, cols = a.shape
    sub = 8
    pad_rows = -(-k // sub) * sub - k
    neg = jnp.full((pad_rows, cols), -jnp.inf, F32)
    a_pad = jnp.concatenate([a, neg], axis=0)
    b_pad = jnp.concatenate([b, neg], axis=0)
    pieces = [a[0:1] + b_pad]
    row = lax.broadcasted_iota(jnp.int32, (sub, cols), 0)
    for i in range(1, sub):
        pieces.append(jnp.where(row < k // (i + 1), a[i:i + 1] + b_pad[0:sub], -jnp.inf))
    pieces.append(a_pad[sub:] + b[0:1])
    return jnp.concatenate(pieces, axis=0)


def _peer_query_kernel(x_ref, g_ref, mod_ref, wq_ref, keys_ref,
                       h2t_ref, s2_ref, e2_ref, thr_ref, e1_ref):
    m = mod_ref[0]
    h = _norm_mod(x_ref[...], g_ref[...], m[3:4], m[4:5])
    hb = h.astype(BF16)
    h2t_ref[...] = h.T.astype(BF16)
    q = jnp.dot(hb, wq_ref[...], preferred_element_type=F32)
    nt = (((1,), (1,)), ((), ()))
    for hd in range(PEER_HEADS):
        qh = q[:, hd * PEER_KEY_DIM:(hd + 1) * PEER_KEY_DIM].astype(BF16)
        s1 = lax.dot_general(keys_ref[2 * hd], qh, nt, preferred_element_type=F32)
        s2 = lax.dot_general(keys_ref[2 * hd + 1], qh, nt, preferred_element_type=F32)
        a = _top_rows(s1, PEER_TOPK + 1)
        b = _top_rows(s2, PEER_TOPK + 1)
        v = _top_rows(_pair_sums(a, b), PEER_TOPK + 1)
        z = jnp.sum(jnp.exp(v[:PEER_TOPK] - v[0:1]), axis=0, keepdims=True)
        tau = 0.5 * (v[PEER_TOPK - 1:PEER_TOPK] + v[PEER_TOPK:PEER_TOPK + 1])
        s2_ref[hd] = s2
        e2_ref[hd] = jnp.exp(s2 - b[0:1])
        thr_ref[hd] = tau - s1
        e1_ref[hd] = jnp.exp(s1 - a[0:1]) / z


def peer_query(x, g, mod, wq, keys_pad, n_ctx_tok, lat_len, tq=256):
    t, d = x.shape
    row = functools.partial(_mod_row, tile=tq, n_ctx_tok=n_ctx_tok, lat_len=lat_len)
    aux = jax.ShapeDtypeStruct((PEER_HEADS, PEER_N_KEYS, t), F32)
    aux_spec = pl.BlockSpec((PEER_HEADS, PEER_N_KEYS, tq), lambda i: (0, 0, i))
    return pl.pallas_call(
        _peer_query_kernel,
        out_shape=(jax.ShapeDtypeStruct((d, t), BF16), aux, aux, aux, aux),
        grid=(t // tq,),
        in_specs=[pl.BlockSpec((tq, d), lambda i: (i, 0)),
                  pl.BlockSpec((1, d), lambda i: (0, 0)),
                  pl.BlockSpec((1, N_MOD, d), lambda i: (row(i), 0, 0)),
                  pl.BlockSpec(wq.shape, lambda i: (0, 0)),
                  pl.BlockSpec(keys_pad.shape, lambda i: (0, 0, 0))],
        out_specs=(pl.BlockSpec((d, tq), lambda i: (0, i)), aux_spec, aux_spec, aux_spec, aux_spec),
        compiler_params=_cparams(("arbitrary",)),
        name="peer_query",
    )(x, g.reshape(1, d), mod, wq, keys_pad)


def _peer_main_kernel(xt_ref, u_ref, vt_ref, s2_ref, e2_ref, thr_ref, e1_ref, o_ref, w_ref, *, n_sub):
    k = pl.program_id(1)

    @pl.when(k == 0)
    def _():
        o_ref[...] = jnp.zeros_like(o_ref)

    for r in range(n_sub):
        i1 = k * n_sub + r
        w = None
        for hd in range(PEER_HEADS):
            thr = thr_ref[hd, pl.ds(i1, 1), :]
            e1 = e1_ref[hd, pl.ds(i1, 1), :]
            term = jnp.where(s2_ref[hd] >= thr, e2_ref[hd], 0.0) * e1
            w = term if w is None else w + term
        w_ref[r * PEER_N_KEYS:(r + 1) * PEER_N_KEYS, :] = w

    hid = jnp.dot(u_ref[...], xt_ref[...], preferred_element_type=F32)
    hw = (_gelu_tanh(hid) * w_ref[...]).astype(BF16)
    o_ref[...] += jnp.dot(vt_ref[...], hw, preferred_element_type=F32)


def peer_main(h2t, u, vt, s2, e2, thr, e1, tt=512, te=512):
    d, t = h2t.shape
    n_exp = u.shape[0]
    once = pl.Buffered(1)
    aux_spec = pl.BlockSpec((PEER_HEADS, PEER_N_KEYS, tt), lambda i, k: (0, 0, i), pipeline_mode=once)
    return pl.pallas_call(
        functools.partial(_peer_main_kernel, n_sub=te // PEER_N_KEYS),
        out_shape=jax.ShapeDtypeStruct((d, t), F32),
        grid=(t // tt, n_exp // te),
        in_specs=[pl.BlockSpec((d, tt), lambda i, k: (0, i), pipeline_mode=once),
                  pl.BlockSpec((te, d), lambda i, k: (k, 0)),
                  pl.BlockSpec((d, te), lambda i, k: (0, k)),
                  aux_spec, aux_spec, aux_spec, aux_spec],
        out_specs=pl.BlockSpec((d, tt), lambda i, k: (0, i), pipeline_mode=once),
        scratch_shapes=[pltpu.VMEM((te, tt), F32)],
        compiler_params=_cparams(("arbitrary", "arbitrary")),
        name="peer_main",
    )(h2t, u, vt, s2, e2, thr, e1)


def _peer_residual_kernel(x_ref, pt_ref, mod_ref, g_ref, o_ref, *, final):
    x = x_ref[...] + mod_ref[0][5:6] * pt_ref[...].T
    if final:
        r = lax.rsqrt(jnp.mean(x * x, axis=-1, keepdims=True) + EPS)
        x = x * r * g_ref[...]
    o_ref[...] = x


def peer_residual(x, peer_t, mod, g_final, n_ctx_tok, lat_len, final, tr=256):
    t, d = x.shape
    row = functools.partial(_mod_row, tile=tr, n_ctx_tok=n_ctx_tok, lat_len=lat_len)
    return pl.pallas_call(
        functools.partial(_peer_residual_kernel, final=final),
        out_shape=jax.ShapeDtypeStruct((t, d), F32),
        grid=(t // tr,),
        in_specs=[pl.BlockSpec((tr, d), lambda i: (i, 0)),
                  pl.BlockSpec((d, tr), lambda i: (0, i)),
                  pl.BlockSpec((1, N_MOD, d), lambda i: (row(i), 0, 0)),
                  pl.BlockSpec((1, d), lambda i: (0, 0))],
        out_specs=pl.BlockSpec((tr, d), lambda i: (i, 0)),
        compiler_params=_cparams(("arbitrary",)),
        name="peer_residual",
    )(x, peer_t, mod, g_final.reshape(1, d))


_NT = (((1,), (1,)), ((), ()))


def _head_rmsnorm(x, g):
    return x * lax.rsqrt(jnp.mean(x * x, axis=-1, keepdims=True) + EPS) * g


def _rope_half(x, cosf, sinf):
    return x * cosf + pltpu.roll(x, GQA_HEAD_DIM // 2, 1) * sinf


def _rope_quarter(x, cosf, sinf):
    lane = lax.broadcasted_iota(jnp.int32, x.shape, 1)
    half = DIFF_QK_DIM // 2
    swapped = jnp.where((lane % DIFF_QK_DIM) < half,
                        pltpu.roll(x, LANES - half, 1), pltpu.roll(x, half, 1))
    return x * cosf + swapped * sinf


def _key_prep_kernel(gk_ref, fk_ref, kn_ref, cg_ref, sg_ref, cd_ref, sd_ref, kg_ref, kd_ref):
    for hh in range(GQA_KV_HEADS):
        sl = slice(hh * GQA_HEAD_DIM, (hh + 1) * GQA_HEAD_DIM)
        kg_ref[:, sl] = _rope_half(_head_rmsnorm(gk_ref[:, sl], kn_ref[...]), cg_ref[...], sg_ref[...])
    for hh in range(DIFF_HEADS):
        sl = slice(hh * LANES, (hh + 1) * LANES)
        kd_ref[:, sl] = _rope_quarter(fk_ref[:, sl], cd_ref[...], sd_ref[...])


def key_prep(z, k_norm, tabs, tm=256):
    t = z.shape[0]
    gk_blk = Z_START['gk'] // Z_WIDTH['gk']
    fk_blk = Z_START['fk'] // Z_WIDTH['fk']
    tab_spec = pl.BlockSpec((tm, LANES), lambda i: (i, 0))
    return pl.pallas_call(
        _key_prep_kernel,
        out_shape=(jax.ShapeDtypeStruct((t, Z_WIDTH['gk']), F32), jax.ShapeDtypeStruct((t, Z_WIDTH['fk']), F32)),
        grid=(t // tm,),
        in_specs=[pl.BlockSpec((tm, Z_WIDTH['gk']), lambda i: (i, gk_blk)),
                  pl.BlockSpec((tm, Z_WIDTH['fk']), lambda i: (i, fk_blk)),
                  pl.BlockSpec((1, GQA_HEAD_DIM), lambda i: (0, 0)),
                  tab_spec, tab_spec, tab_spec, tab_spec],
        out_specs=(pl.BlockSpec((tm, Z_WIDTH['gk']), lambda i: (i, 0)),
                   pl.BlockSpec((tm, Z_WIDTH['fk']), lambda i: (i, 0))),
        compiler_params=_cparams(("arbitrary",)),
        name="key_prep",
    )(z, z, k_norm.reshape(1, -1), *tabs)


def _softmax_rows(s):
    m = jnp.max(s, axis=-1, keepdims=True)
    p = jnp.exp(s - m)
    return p, jnp.sum(p, axis=-1, keepdims=True)


def _gqa_kernel(q_ref, qn_ref, cos_ref, sin_ref, k_ref, v_ref, o_ref):
    tq = q_ref.shape[0]
    n_rep = GQA_HEADS // GQA_KV_HEADS
    qs = []
    for hh in range(n_rep):
        x = _head_rmsnorm(q_ref[:, hh * GQA_HEAD_DIM:(hh + 1) * GQA_HEAD_DIM], qn_ref[...])
        x = _rope_half(x, cos_ref[...], sin_ref[...])
        qs.append((x * (GQA_HEAD_DIM ** -0.5)).astype(BF16))
    q = jnp.concatenate(qs, axis=0)
    s = lax.dot_general(q, k_ref[0], _NT, preferred_element_type=F32)
    p, l = _softmax_rows(s)
    o = jnp.dot(p.astype(BF16), v_ref[0], preferred_element_type=F32) / l
    o_ref[...] = jnp.concatenate([o[hh * tq:(hh + 1) * tq] for hh in range(n_rep)], axis=1).astype(o_ref.dtype)


def gqa_attention(z, q_norm, cos, sin, keys, vals, tok0, seq_len, tq):
    b, lk, _ = keys.shape
    n_rep = GQA_HEADS // GQA_KV_HEADS
    qw = n_rep * GQA_HEAD_DIM
    q_blk = Z_START['gq'] // qw
    nq = seq_len // tq
    row = lambda bi, g, qi: (tok0 + bi * seq_len) // tq + qi
    kv_spec = pl.BlockSpec((1, lk, GQA_HEAD_DIM), lambda bi, g, qi: (bi, 0, g))
    tab_spec = pl.BlockSpec((tq, LANES), lambda bi, g, qi: (row(bi, g, qi), 0))
    return pl.pallas_call(
        _gqa_kernel,
        out_shape=jax.ShapeDtypeStruct((b * seq_len, GROUP_W), BF16),
        grid=(b, GQA_KV_HEADS, nq),
        in_specs=[pl.BlockSpec((tq, qw), lambda bi, g, qi: (row(bi, g, qi), q_blk + g)),
                  pl.BlockSpec((1, GQA_HEAD_DIM), lambda bi, g, qi: (0, 0)),
                  tab_spec, tab_spec, kv_spec, kv_spec],
        out_specs=pl.BlockSpec((tq, qw), lambda bi, g, qi: (bi * nq + qi, g)),
        compiler_params=_cparams(("arbitrary", "arbitrary", "arbitrary")),
        name="gqa_attention",
    )(z, q_norm.reshape(1, -1), cos, sin, keys, vals)


def _diff_kernel(q_ref, cos_ref, sin_ref, k_ref, v_ref, lp_ref, g_ref, o_ref, *, lam_init):
    tq = q_ref.shape[0]
    lp = lp_ref[...]
    lam = (jnp.exp(jnp.sum(lp[0:1] * lp[1:2], axis=1, keepdims=True))
           - jnp.exp(jnp.sum(lp[2:3] * lp[3:4], axis=1, keepdims=True)) + lam_init)
    q = _rope_quarter(q_ref[...], cos_ref[...], sin_ref[...]) * (DIFF_QK_DIM ** -0.5)
    lane = lax.broadcasted_iota(jnp.int32, q.shape, 1)
    q12 = jnp.concatenate([jnp.where(lane < DIFF_QK_DIM, q, 0.0), jnp.where(lane >= DIFF_QK_DIM, q, 0.0)], axis=0)
    s = lax.dot_general(q12.astype(BF16), k_ref[0], _NT, preferred_element_type=F32)
    p, l = _softmax_rows(s)
    p = p / l
    w = p[:tq] - lam * p[tq:]
    o = jnp.dot(w.astype(BF16), v_ref[0], preferred_element_type=F32)
    o_ref[...] = (_head_rmsnorm(o, g_ref[...]) * (1.0 - lam_init)).astype(o_ref.dtype)


def diff_attention(z, cos, sin, keys, vals, lam_params, subln_g, layer_idx, tok0, seq_len, tq):
    b, lk, _ = keys.shape
    q_blk = Z_START['fq'] // LANES
    nq = seq_len // tq
    lam_init = 0.8 - 0.6 * math.exp(-0.3 * layer_idx)
    row = lambda bi, h, qi: (tok0 + bi * seq_len) // tq + qi
    kv_spec = pl.BlockSpec((1, lk, LANES), lambda bi, h, qi: (bi, 0, h))
    tab_spec = pl.BlockSpec((tq, LANES), lambda bi, h, qi: (row(bi, h, qi), 0))
    return pl.pallas_call(
        functools.partial(_diff_kernel, lam_init=lam_init),
        out_shape=jax.ShapeDtypeStruct((b * seq_len, GROUP_W), BF16),
        grid=(b, DIFF_HEADS, nq),
        in_specs=[pl.BlockSpec((tq, LANES), lambda bi, h, qi: (row(bi, h, qi), q_blk + h)),
                  tab_spec, tab_spec, kv_spec, kv_spec,
                  pl.BlockSpec(lam_params.shape, lambda bi, h, qi: (0, 0)),
                  pl.BlockSpec((1, DIFF_V_DIM), lambda bi, h, qi: (0, 0))],
        out_specs=pl.BlockSpec((tq, LANES), lambda bi, h, qi: (bi * nq + qi, h)),
        compiler_params=_cparams(("arbitrary", "arbitrary", "arbitrary")),
        name="diff_attention",
    )(z, cos, sin, keys, vals, lam_params, subln_g.reshape(1, -1))


def _rope_lane_tables(length, dim, n_ctx_tok, dec_batch):
    cos, sin = _rope_tables(length, dim)
    reps = LANES // dim
    cosf = jnp.tile(jnp.concatenate([cos, cos], axis=1), (dec_batch, reps))
    sinf = jnp.tile(jnp.concatenate([-sin, sin], axis=1), (dec_batch, reps))
    ones = jnp.ones((n_ctx_tok, LANES), F32)
    return jnp.concatenate([ones, cosf], axis=0), jnp.concatenate([jnp.zeros_like(ones), sinf], axis=0)


S5_BLK_GROUPS = LANES // S5_CH_PER_GROUP
S5_BLOCKS = S5_GROUPS // S5_BLK_GROUPS
S5_BLK_STATE = S5_BLK_GROUPS * S5_STATE
SUBLANES = 8


def _s5_tables(lam_re, lam_im, log_dt, b_re, b_im, c_re, c_im):
    dt = jnp.exp(log_dt)[..., None]
    mag = jnp.exp(lam_re * dt)
    ab_re, ab_im = mag * jnp.cos(lam_im * dt), mag * jnp.sin(lam_im * dt)
    den = lam_re * lam_re + lam_im * lam_im
    nr = ab_re - 1.0
    coef_re = (nr * lam_re + ab_im * lam_im) / den
    coef_im = (ab_im * lam_re - nr * lam_im) / den
    bb_re = coef_re[..., None] * b_re - coef_im[..., None] * b_im
    bb_im = coef_re[..., None] * b_im + coef_im[..., None] * b_re
    j, gl, p, h = S5_BLOCKS, S5_BLK_GROUPS, S5_STATE, S5_CH_PER_GROUP
    eye = jnp.eye(gl, dtype=F32)

    def in_blocks(bb):
        t = bb.reshape(2, j, gl, p, h)
        return jnp.einsum('djgph,gk->djghkp', t, eye).reshape(2, j, gl * h, gl * p)

    def out_blocks(cc):
        t = cc.reshape(2, j, gl, h, p)
        return jnp.einsum('djghp,gk->djgpkh', t, eye).reshape(2, j, gl * p, gl * h)

    wb = jnp.concatenate([in_blocks(bb_re), in_blocks(bb_im)], axis=-1).astype(BF16)
    wc = jnp.concatenate([out_blocks(c_re), out_blocks(-c_im)], axis=-2).astype(BF16)

    ar, ai = ab_re.reshape(2, j, 1, gl * p), ab_im.reshape(2, j, 1, gl * p)
    pows = [(ar, ai)]
    for _ in range(SUBLANES - 1):
        pr, pi = pows[-1]
        pows.append((pr * ar - pi * ai, pr * ai + pi * ar))
    r = jnp.arange(SUBLANES).reshape(1, 1, SUBLANES, 1)
    rows = []
    for d in range(2):
        dr = []
        for s in (1, 2, 4):
            mask = (r >= s) if d == 0 else (r <= SUBLANES - 1 - s)
            dr += [jnp.where(mask, pows[s - 1][0][d:d + 1], 0.0), jnp.where(mask, pows[s - 1][1][d:d + 1], 0.0)]
        order = range(SUBLANES) if d == 0 else range(SUBLANES - 1, -1, -1)
        dr.append(jnp.concatenate([pows[k][0][d:d + 1] for k in order], axis=2))
        dr.append(jnp.concatenate([pows[k][1][d:d + 1] for k in order], axis=2))
        rows.append(jnp.stack([jnp.broadcast_to(x, (1, j, SUBLANES, gl * p)) for x in dr], axis=2))
    return wb, wc, jnp.concatenate(rows, axis=0)


def _s5_scan_tile(xr, xi, coef_ref, d, hr, hi):
    for k, s in enumerate((1, 2, 4)):
        shift = s if d == 0 else SUBLANES - s
        ar, ai = coef_ref[d, 0, 2 * k], coef_ref[d, 0, 2 * k + 1]
        sr, si = pltpu.roll(xr, shift, 0), pltpu.roll(xi, shift, 0)
        xr, xi = xr + ar * sr - ai * si, xi + ar * si + ai * sr
    cr, ci = coef_ref[d, 0, 6], coef_ref[d, 0, 7]
    return xr + cr * hr - ci * hi, xi + cr * hi + ci * hr


def _s5_scan_kernel(uf_ref, ub_ref, wb_ref, wc_ref, coef_ref, h0_ref, yf_ref, yb_ref, hfin_ref,
                    sr_ref, si_ref, carry_ref):
    c = pl.program_id(2)
    bs = S5_BLK_STATE
    tc = uf_ref.shape[0]
    n_tiles = tc // SUBLANES

    @pl.when(c == 0)
    def _():
        carry_ref[...] = h0_ref[0, :, 0]

    for d, (u_ref, y_ref) in enumerate(((uf_ref, yf_ref), (ub_ref, yb_ref))):
        bu = jnp.dot(u_ref[...].astype(BF16), wb_ref[d, 0], preferred_element_type=F32)
        sr_ref[...] = bu[:, :bs]
        si_ref[...] = bu[:, bs:]

        def body(i, carry, d=d):
            hr, hi = carry
            tile = i if d == 0 else n_tiles - 1 - i
            rows = pl.ds(pl.multiple_of(tile * SUBLANES, SUBLANES), SUBLANES)
            xr, xi = _s5_scan_tile(sr_ref[rows, :], si_ref[rows, :], coef_ref, d, hr, hi)
            sr_ref[rows, :] = xr
            si_ref[rows, :] = xi
            last = SUBLANES - 1 if d == 0 else 0
            return xr[last:last + 1], xi[last:last + 1]

        h0 = carry_ref[d]
        hr, hi = lax.fori_loop(0, n_tiles, body, (h0[:, :bs], h0[:, bs:]))
        carry_ref[d] = jnp.concatenate([hr, hi], axis=1)
        hcat = jnp.concatenate([sr_ref[...], si_ref[...]], axis=1).astype(BF16)
        y_ref[...] = jnp.dot(hcat, wc_ref[d, 0], preferred_element_type=F32)

    hfin_ref[0, :, 0] = carry_ref[...]


def s5_scan(z, wb, wc, coef, h0, tok0, seq_len, tc):
    b = h0.shape[0]
    nt = seq_len // tc
    width = 2 * S5_BLK_STATE
    u_blk = Z_START['s5_u'] // LANES
    row_f = lambda bi, j, c: ((tok0 + bi * seq_len) // tc + c, u_blk + j)
    row_b = lambda bi, j, c: ((tok0 + bi * seq_len) // tc + nt - 1 - c, u_blk + j)
    st_spec = pl.BlockSpec((1, 2, 1, 1, width), lambda bi, j, c: (bi, 0, j, 0, 0))
    y_shape = jax.ShapeDtypeStruct((b * seq_len, S5_WIDTH), F32)
    return pl.pallas_call(
        _s5_scan_kernel,
        out_shape=(y_shape, y_shape, jax.ShapeDtypeStruct(h0.shape, F32)),
        grid=(b, S5_BLOCKS, nt),
        in_specs=[pl.BlockSpec((tc, LANES), row_f),
                  pl.BlockSpec((tc, LANES), row_b),
                  pl.BlockSpec((2, 1, LANES, width), lambda bi, j, c: (0, j, 0, 0)),
                  pl.BlockSpec((2, 1, width, LANES), lambda bi, j, c: (0, j, 0, 0)),
                  pl.BlockSpec((2, 1, SUBLANES, SUBLANES, S5_BLK_STATE), lambda bi, j, c: (0, j, 0, 0, 0)),
                  st_spec],
        out_specs=(pl.BlockSpec((tc, LANES), lambda bi, j, c: (bi * nt + c, j)),
                   pl.BlockSpec((tc, LANES), lambda bi, j, c: (bi * nt + nt - 1 - c, j)),
                   st_spec),
        scratch_shapes=[pltpu.VMEM((tc, S5_BLK_STATE), F32), pltpu.VMEM((tc, S5_BLK_STATE), F32),
                        pltpu.VMEM((2, 1, width), F32)],
        compiler_params=_cparams(("arbitrary", "arbitrary", "arbitrary")),
        name="s5_scan",
    )(z, z, wb, wc, coef, h0)


def _s5_glu_kernel(yf_ref, yb_ref, u_ref, d_ref, w_ref, o_ref):
    y = _gelu_tanh(yf_ref[...] + yb_ref[...] + d_ref[...] * u_ref[...])
    gate = jnp.dot(y.astype(BF16), w_ref[...], preferred_element_type=F32)
    o_ref[...] = (y * (1.0 / (1.0 + jnp.exp(-gate)))).astype(o_ref.dtype)


def s5_glu(yf, yb, z, d_skip, w_glu, tm=512):
    t = z.shape[0]
    blk = pl.BlockSpec((tm, S5_WIDTH), lambda i: (i, 0))
    return pl.pallas_call(
        _s5_glu_kernel,
        out_shape=jax.ShapeDtypeStruct((t, S5_WIDTH), BF16),
        grid=(t // tm,),
        in_specs=[blk, blk, pl.BlockSpec((tm, S5_WIDTH), lambda i: (i, Z_START['s5_u'] // S5_WIDTH)),
                  pl.BlockSpec((1, S5_WIDTH), lambda i: (0, 0)),
                  pl.BlockSpec((S5_WIDTH, S5_WIDTH), lambda i: (0, 0))],
        out_specs=blk,
        compiler_params=_cparams(("arbitrary",)),
        name="s5_glu",
    )(yf, yb, z, d_skip.reshape(1, -1), w_glu)


def _s5_state_in(re, im):
    b = re.shape[0]
    r = re.reshape(b, 2, S5_BLOCKS, 1, S5_BLK_STATE)
    i = im.reshape(b, 2, S5_BLOCKS, 1, S5_BLK_STATE)
    return jnp.concatenate([r, i], axis=-1)


def _s5_state_out(h):
    b = h.shape[0]
    re = h[..., :S5_BLK_STATE].reshape(b, 2, S5_GROUPS, S5_STATE)
    im = h[..., S5_BLK_STATE:].reshape(b, 2, S5_GROUPS, S5_STATE)
    return re, im


def _dn_conv_kernel(x_ref, prev_ref, next_ref, w_ref, o_ref, *, n_ctx_tok, seq, lat_len):
    i = pl.program_id(0)
    cg = pl.program_id(1)
    tm = x_ref.shape[0]
    t0 = i * tm
    in_ctx = t0 < n_ctx_tok
    pos = jnp.where(in_ctx, t0 % seq, (t0 - n_ctx_tok) % lat_len)
    length = jnp.where(in_ctx, seq, lat_len)
    x = x_ref[...]
    row = lax.broadcasted_iota(jnp.int32, x.shape, 0)
    before = jnp.where(pos > 0, prev_ref[SUBLANES - 1:SUBLANES, :], 0.0)
    after = jnp.where(pos + tm < length, next_ref[0:1, :], 0.0)
    xp = jnp.where(row == 0, before, pltpu.roll(x, 1, 0))
    xn = jnp.where(row == tm - 1, after, pltpu.roll(x, tm - 1, 0))
    y = _silu(w_ref[0:1, :] * xp + w_ref[1:2, :] * x + w_ref[2:3, :] * xn)
    scale = jnp.where(cg == 0, DN_HEAD_DIM ** -0.5, 1.0)
    for hh in range(DN_HEADS):
        sl = slice(hh * DN_HEAD_DIM, (hh + 1) * DN_HEAD_DIM)
        yh = y[:, sl]
        nrm = yh * (lax.rsqrt(jnp.sum(yh * yh, axis=-1, keepdims=True) + EPS) * scale)
        o_ref[:, sl] = jnp.where(cg == 2, yh, nrm)


def dn_conv(z, conv_w, n_ctx_tok, seq, lat_len, tm=256):
    t = z.shape[0]
    first = Z_START['dn_q'] // GROUP_W
    nsub = tm // SUBLANES
    last_sub = t // SUBLANES - 1
    return pl.pallas_call(
        functools.partial(_dn_conv_kernel, n_ctx_tok=n_ctx_tok, seq=seq, lat_len=lat_len),
        out_shape=jax.ShapeDtypeStruct((t, 3 * GROUP_W), F32),
        grid=(t // tm, 3),
        in_specs=[pl.BlockSpec((tm, GROUP_W), lambda i, cg: (i, first + cg)),
                  pl.BlockSpec((SUBLANES, GROUP_W), lambda i, cg: (jnp.maximum(i * nsub - 1, 0), first + cg)),
                  pl.BlockSpec((SUBLANES, GROUP_W), lambda i, cg: (jnp.minimum((i + 1) * nsub, last_sub), first + cg)),
                  pl.BlockSpec((3, GROUP_W), lambda i, cg: (0, cg))],
        out_specs=pl.BlockSpec((tm, GROUP_W), lambda i, cg: (i, cg)),
        compiler_params=_cparams(("arbitrary", "arbitrary")),
        name="dn_conv",
    )(z, z, z, conv_w)


def _dn_gate_kernel(x_ref, neg_a_ref, dtb_ref, o_ref):
    xt = x_ref[...].T
    nh = 2 * DN_HEADS
    a = xt[0:nh] + dtb_ref[...]
    softplus = jnp.maximum(a, 0.0) + jnp.log(1.0 + jnp.exp(-jnp.abs(a)))
    o_ref[0:nh, :] = neg_a_ref[...] * softplus
    o_ref[nh:2 * nh, :] = 1.0 / (1.0 + jnp.exp(-xt[nh:2 * nh]))


def dn_gates(z, a_log, dt_bias, tm=256):
    t = z.shape[0]
    nh = 2 * DN_HEADS
    neg_a = jnp.broadcast_to(-jnp.exp(a_log).reshape(nh, 1), (nh, tm))
    dtb = jnp.broadcast_to(dt_bias.reshape(nh, 1), (nh, tm))
    return pl.pallas_call(
        _dn_gate_kernel,
        out_shape=jax.ShapeDtypeStruct((2 * nh, t), F32),
        grid=(t // tm,),
        in_specs=[pl.BlockSpec((tm, LANES), lambda i: (i, Z_START['dn_a'] // LANES)),
                  pl.BlockSpec((nh, tm), lambda i: (0, 0)),
                  pl.BlockSpec((nh, tm), lambda i: (0, 0))],
        out_specs=pl.BlockSpec((2 * nh, tm), lambda i: (0, i)),
        compiler_params=_cparams(("arbitrary",)),
        name="dn_gates",
    )(z, neg_a, dtb)


def _dn_masks(c, upper):
    rows = lax.broadcasted_iota(jnp.int32, (c, c), 0)
    cols = lax.broadcasted_iota(jnp.int32, (c, c), 1)
    levels = []
    s = 1
    while s < c:
        same = (rows ^ cols) < 2 * s
        lo, hi = (rows & s) != 0, (cols & s) == 0
        if upper:
            lo, hi = (rows & s) == 0, (cols & s) != 0
        levels.append(jnp.where(same, jnp.where(lo, jnp.where(hi, 1.0, 0.0), 0.0), 0.0))
        s *= 2
    return dict(eye=rows == cols,
                causal=(rows <= cols) if upper else (rows >= cols),
                causal_t=(rows >= cols) if upper else (rows <= cols),
                strict=(rows < cols) if upper else (rows > cols),
                levels=levels)


def _mm(a, b):
    return jnp.dot(a.astype(BF16), b.astype(BF16), preferred_element_type=F32)


def _mm_nt(a, b):
    return lax.dot_general(a.astype(BF16), b.astype(BF16), _NT, preferred_element_type=F32)


def _dn_chunk_local(q, k, v, g_row, beta_row, masks):
    eye, causal, causal_t = masks['eye'], masks['causal'], masks['causal_t']
    g_col = jnp.sum(jnp.where(eye, g_row, 0.0), axis=1, keepdims=True)
    beta_col = jnp.sum(jnp.where(eye, beta_row, 0.0), axis=1, keepdims=True)
    gc_col = jnp.sum(jnp.where(causal, g_row, 0.0), axis=1, keepdims=True)
    gc_row = jnp.sum(jnp.where(causal_t, g_col, 0.0), axis=0, keepdims=True)
    g_last = jnp.sum(g_row, axis=1, keepdims=True)
    kb = k * beta_col
    return dict(q=q, k=k, kb=kb, masks=masks,
                decay=jnp.where(causal, jnp.exp(jnp.where(causal, gc_col - gc_row, 0.0)), 0.0),
                rhs=jnp.concatenate([v * beta_col, kb * jnp.exp(gc_col)], axis=1),
                q_dec=q * jnp.exp(gc_col),
                k_dec_t=(k * jnp.exp(g_last - gc_col)).T,
                s_decay=jnp.exp(g_last))


def _dn_solve_chunks(chunks):
    for ch in chunks:
        m = ch['masks']
        ch['a'] = jnp.where(m['strict'], _mm_nt(ch['kb'], ch['k']) * ch['decay'], 0.0)
        ch['t'] = jnp.where(m['eye'], 1.0, 0.0) - ch['a'] * m['levels'][0]
    for lvl in range(1, len(chunks[0]['masks']['levels'])):
        for ch in chunks:
            ch['mid'] = _mm(ch['a'] * ch['masks']['levels'][lvl], ch['t'])
        for ch in chunks:
            ch['t'] = ch['t'] - _mm(ch['t'], ch['mid'])
    for ch in chunks:
        sol = _mm(ch['t'], ch['rhs'])
        dv = sol.shape[1] // 2
        ch['u'], ch['w'] = sol[:, :dv], sol[:, dv:]
    for ch in chunks:
        ch['attn'] = _mm_nt(ch['q'], ch['k']) * ch['decay']


def _dn_advance(chunks, states):
    w_s = [_mm(ch['w'], s) for ch, s in zip(chunks, states)]
    q_s = [_mm(ch['q_dec'], s) for ch, s in zip(chunks, states)]
    v_new = [ch['u'] - ws for ch, ws in zip(chunks, w_s)]
    outs = [qs + _mm(ch['attn'], vn) for ch, qs, vn in zip(chunks, q_s, v_new)]
    new_states = [s * ch['s_decay'] + _mm(ch['k_dec_t'], vn) for ch, s, vn in zip(chunks, states, v_new)]
    return outs, new_states


def _dn_chunk_kernel(qf_ref, kf_ref, vf_ref, gf_ref, qb_ref, kb_ref, vb_ref, gb_ref, s0_ref,
                     of_ref, ob_ref, sfin_ref, s_ref):
    hg = pl.program_id(1)
    c = pl.program_id(2)
    hb = s_ref.shape[1]
    hdim = DN_HEAD_DIM

    @pl.when(c == 0)
    def _():
        s_ref[...] = s0_ref[0]

    n_sub = qf_ref.shape[0] // DN_CHUNK
    nh = 2 * DN_HEADS
    dirs = ((qf_ref, kf_ref, vf_ref, gf_ref, of_ref), (qb_ref, kb_ref, vb_ref, gb_ref, ob_ref))
    masks = [_dn_masks(DN_CHUNK, upper=False), _dn_masks(DN_CHUNK, upper=True)]
    chains = [(d, hh) for d in range(2) for hh in range(hb)]
    local = {}
    for d, hh in chains:
        q_ref, k_ref, v_ref, g_ref, _ = dirs[d]
        head = hg * hb + hh
        lanes = slice(hh * hdim, (hh + 1) * hdim)
        g_all = g_ref[pl.ds(d * DN_HEADS + head, 1), :]
        beta_all = g_ref[pl.ds(nh + d * DN_HEADS + head, 1), :]
        for j in range(n_sub):
            tok = slice(j * DN_CHUNK, (j + 1) * DN_CHUNK)
            local[d, hh, j] = _dn_chunk_local(q_ref[tok, lanes], k_ref[tok, lanes], v_ref[tok, lanes],
                                              g_all[:, tok], beta_all[:, tok], masks[d])
    _dn_solve_chunks(list(local.values()))
    states = [s_ref[d, hh] for d, hh in chains]
    for step in range(n_sub):
        subs = [step if d == 0 else n_sub - 1 - step for d, _ in chains]
        outs, states = _dn_advance([local[d, hh, j] for (d, hh), j in zip(chains, subs)], states)
        for (d, hh), j, o in zip(chains, subs, outs):
            dirs[d][4][j * DN_CHUNK:(j + 1) * DN_CHUNK, hh * hdim:(hh + 1) * hdim] = o
    for (d, hh), s in zip(chains, states):
        s_ref[d, hh] = s
    sfin_ref[0] = s_ref[...]


def dn_chunk(qkv, gates, s0, tok0, seq_len, blk=128, hb=4):
    b = s0.shape[0]
    nb = seq_len // blk
    hdim = DN_HEAD_DIM
    n_hg = DN_HEADS // hb
    fwd = lambda bi, c: (tok0 + bi * seq_len) // blk + c
    bwd = lambda bi, c: (tok0 + bi * seq_len) // blk + nb - 1 - c

    def col_spec(pos, o):
        return pl.BlockSpec((blk, hb * hdim), lambda bi, hg, c: (pos(bi, c), o * n_hg + hg))

    def specs(pos):
        return [col_spec(pos, o) for o in range(3)] + [
            pl.BlockSpec((4 * DN_HEADS, blk), lambda bi, hg, c: (0, pos(bi, c)))]

    st_spec = pl.BlockSpec((1, 2, hb, hdim, hdim), lambda bi, hg, c: (bi, 0, hg, 0, 0))
    o_shape = jax.ShapeDtypeStruct((b * seq_len, GROUP_W), F32)
    return pl.pallas_call(
        _dn_chunk_kernel,
        out_shape=(o_shape, o_shape, jax.ShapeDtypeStruct(s0.shape, F32)),
        grid=(b, n_hg, nb),
        in_specs=specs(fwd) + specs(bwd) + [st_spec],
        out_specs=(pl.BlockSpec((blk, hb * hdim), lambda bi, hg, c: (bi * nb + c, hg)),
                   pl.BlockSpec((blk, hb * hdim), lambda bi, hg, c: (bi * nb + nb - 1 - c, hg)),
                   st_spec),
        scratch_shapes=[pltpu.VMEM((2, hb, hdim, hdim), F32)],
        compiler_params=_cparams(("arbitrary", "arbitrary", "arbitrary")),
        name="dn_chunk",
    )(qkv, qkv, qkv, gates, qkv, qkv, qkv, gates, s0)


def _dn_post_kernel(of_ref, ob_ref, gate_ref, g_ref, o_ref):
    for hh in range(DN_HEADS):
        sl = slice(hh * DN_HEAD_DIM, (hh + 1) * DN_HEAD_DIM)
        o = _head_rmsnorm(of_ref[:, sl] + ob_ref[:, sl], g_ref[...])
        o_ref[:, sl] = (o * _silu(gate_ref[:, sl])).astype(o_ref.dtype)


def dn_post(o_f, o_b, z, norm_g, tm=512):
    t = z.shape[0]
    blk = pl.BlockSpec((tm, GROUP_W), lambda i: (i, 0))
    return pl.pallas_call(
        _dn_post_kernel,
        out_shape=jax.ShapeDtypeStruct((t, GROUP_W), BF16),
        grid=(t // tm,),
        in_specs=[blk, blk, pl.BlockSpec((tm, GROUP_W), lambda i: (i, Z_START['dn_gate'] // GROUP_W)),
                  pl.BlockSpec((1, DN_HEAD_DIM), lambda i: (0, 0))],
        out_specs=blk,
        compiler_params=_cparams(("arbitrary",)),
        name="dn_post",
    )(o_f, o_b, z, norm_g.reshape(1, -1))


def _rope_tables(length, dim):
    n_rows = length // GRID_W
    row = jnp.repeat(jnp.arange(n_rows), GRID_W).astype(F32)
    col = jnp.tile(jnp.arange(GRID_W), n_rows).astype(F32)
    quarter = dim // 4
    freqs = ROPE_THETA ** (-jnp.arange(quarter, dtype=F32) / quarter)
    ang = jnp.concatenate([row[:, None] * freqs, col[:, None] * freqs], axis=-1)
    return jnp.cos(ang), jnp.sin(ang)


def _permute_w_in(w):
    cols = [w[:, Z_REF_START[n]:Z_REF_START[n] + Z_WIDTH[n]] for n in _Z_ORDER]
    used = sum(Z_WIDTH[n] for n in _Z_ORDER)
    cols.append(jnp.zeros((w.shape[0], Z_COLS - used), w.dtype))
    return jnp.concatenate(cols, axis=1).astype(BF16)


def _pad_keys(keys):
    h, two, n, half = keys.shape
    z = jnp.zeros((h, n, half), keys.dtype)
    k0 = jnp.concatenate([keys[:, 0], z], axis=-1)
    k1 = jnp.concatenate([z, keys[:, 1]], axis=-1)
    return jnp.stack([k0, k1], axis=1).reshape(2 * h, n, 2 * half).astype(BF16)


def kernel(x_prompt, x_sample, c, cache_gqa_k, cache_gqa_v, cache_diff_k, cache_diff_v, state_s5_re, state_s5_im, state_delta, c_ctx, w_mod, b_mod, norm1_g, norm2_g, w_in, s5_lambda_re, s5_lambda_im, s5_log_dt, s5_b_re, s5_b_im, s5_c_re, s5_c_im, s5_d, s5_w_glu, dn_conv_w, dn_a_log, dn_dt_bias, dn_norm_g, gqa_q_norm, gqa_k_norm, diff_lambda, diff_subln_g, w_out, peer_w_q, peer_keys, peer_u, peer_v, final_norm_g):
    batch, seq, d = x_prompt.shape
    dec_batch, lat_len, _ = x_sample.shape
    depth = w_in.shape[0]
    n_ctx_tok = batch * seq
    n_lat_tok = dec_batch * lat_len
    tab_g = _rope_lane_tables(lat_len, GQA_HEAD_DIM, n_ctx_tok, dec_batch)
    tab_d = _rope_lane_tables(lat_len, DIFF_QK_DIM, n_ctx_tok, dec_batch)
    kv_w = GQA_KV_HEADS * GQA_HEAD_DIM

    def zcol(z, name, lo, hi):
        return z[lo:hi, Z_START[name]:Z_START[name] + Z_WIDTH[name]]

    def with_cache(cache, new, width):
        return jnp.concatenate([cache.reshape(dec_batch, -1, width), new.reshape(dec_batch, lat_len, width)],
                               axis=1).astype(BF16)

    cond = jnp.concatenate([c_ctx[None], c, jnp.zeros((8 - 1 - dec_batch, d), F32)], axis=0)
    mod_all = modulation(cond, w_mod, b_mod).reshape(depth, 8, N_MOD, d)

    x = jnp.concatenate([x_prompt.reshape(n_ctx_tok, d), x_sample.reshape(dec_batch * lat_len, d)], axis=0)
    new_state = [[] for _ in range(7)]
    for l in range(depth):
        mod = mod_all[l]
        z = in_projection(x, norm1_g[l], mod, _permute_w_in(w_in[l]), n_ctx_tok, lat_len)

        wb, wc, coef = _s5_tables(s5_lambda_re[l], s5_lambda_im[l], s5_log_dt[l],
                                  s5_b_re[l], s5_b_im[l], s5_c_re[l], s5_c_im[l])
        h0_ctx = jnp.zeros((batch, 2, S5_BLOCKS, 1, 2 * S5_BLK_STATE), F32)
        yf_c, yb_c, s5_fin = s5_scan(z, wb, wc, coef, h0_ctx, 0, seq, tc=seq)
        yf_l, yb_l, _ = s5_scan(z, wb, wc, coef, _s5_state_in(state_s5_re[:, l], state_s5_im[:, l]),
                                n_ctx_tok, lat_len, tc=256)
        y_a = s5_glu(jnp.concatenate([yf_c, yf_l], axis=0), jnp.concatenate([yb_c, yb_l], axis=0),
                     z, s5_d[l], s5_w_glu[l].astype(BF16))
        s5_re, s5_im = _s5_state_out(s5_fin)

        qkv = dn_conv(z, dn_conv_w[l], n_ctx_tok, seq, lat_len)
        gates = dn_gates(z, dn_a_log[l], dn_dt_bias[l])
        s0_ctx = jnp.zeros((batch, 2, DN_HEADS, DN_HEAD_DIM, DN_HEAD_DIM), F32)
        of_c, ob_c, delta = dn_chunk(qkv, gates, s0_ctx, 0, seq)
        of_l, ob_l, _ = dn_chunk(qkv, gates, state_delta[:, l], n_ctx_tok, lat_len)
        y_b = dn_post(jnp.concatenate([of_c, of_l], axis=0), jnp.concatenate([ob_c, ob_l], axis=0),
                      z, dn_norm_g[l])

        kg, kd = key_prep(z, gqa_k_norm[l], (*tab_g, *tab_d))
        gk_ctx, gv_ctx = kg[:n_ctx_tok], zcol(z, 'gv', 0, n_ctx_tok)
        fk_ctx, fv_ctx = kd[:n_ctx_tok], zcol(z, 'fv', 0, n_ctx_tok)
        yc_c = gqa_attention(z, gqa_q_norm[l], *tab_g, gk_ctx.reshape(batch, seq, kv_w).astype(BF16),
                             gv_ctx.reshape(batch, seq, kv_w).astype(BF16), 0, seq, tq=seq)
        yc_l = gqa_attention(z, gqa_q_norm[l], *tab_g, with_cache(cache_gqa_k[:, l], kg[n_ctx_tok:], kv_w),
                             with_cache(cache_gqa_v[:, l], zcol(z, 'gv', n_ctx_tok, None), kv_w),
                             n_ctx_tok, lat_len, tq=128)
        yd_c = diff_attention(z, *tab_d, fk_ctx.reshape(batch, seq, GROUP_W).astype(BF16),
                              fv_ctx.reshape(batch, seq, GROUP_W).astype(BF16),
                              diff_lambda[l], diff_subln_g[l], l, 0, seq, tq=seq)
        yd_l = diff_attention(z, *tab_d, with_cache(cache_diff_k[:, l], kd[n_ctx_tok:], GROUP_W),
                              with_cache(cache_diff_v[:, l], zcol(z, 'fv', n_ctx_tok, None), GROUP_W),
                              diff_lambda[l], diff_subln_g[l], l, n_ctx_tok, lat_len, tq=256)
        y_c = jnp.concatenate([yc_c, yc_l], axis=0)
        y_d = jnp.concatenate([yd_c, yd_l], axis=0)
        ctx_out = (gk_ctx.reshape(batch, seq, GQA_KV_HEADS, GQA_HEAD_DIM),
                   gv_ctx.reshape(batch, seq, GQA_KV_HEADS, GQA_HEAD_DIM),
                   fk_ctx.reshape(batch, seq, DIFF_HEADS, 2, DIFF_QK_DIM),
                   fv_ctx.reshape(batch, seq, DIFF_HEADS, DIFF_V_DIM),
                   s5_re, s5_im, delta)

        x = out_projection((y_a, y_b, y_c, y_d), w_out[l].astype(BF16), x, mod, n_ctx_tok, lat_len)
        h2, s2, e2, thr, e1 = peer_query(x, norm2_g[l], mod, peer_w_q[l].astype(BF16), _pad_keys(peer_keys[l]),
                                         n_ctx_tok, lat_len)
        peer_t = peer_main(h2, peer_u[l].astype(BF16), peer_v[l].astype(BF16).T, s2, e2, thr, e1)
        x = peer_residual(x, peer_t, mod, final_norm_g, n_ctx_tok, lat_len, final=(l == depth - 1))
        for acc, val in zip(new_state, ctx_out):
            acc.append(val)
    y_prompt = x[:n_ctx_tok].reshape(batch, seq, d)
    y_sample = x[n_ctx_tok:].reshape(dec_batch, lat_len, d)
    return (y_prompt, y_sample) + tuple(jnp.stack(s, axis=1) for s in new_state)
```

```python
import functools
import math

import jax
import jax.numpy as jnp
from jax import lax
from jax.experimental import pallas as pl
from jax.experimental.pallas import tpu as pltpu

F32 = jnp.float32
BF16 = jnp.bfloat16

D_MODEL = 4096
GRID_W = 64
GROUP_W = D_MODEL // 4
S5_WIDTH = GROUP_W
S5_CH_PER_GROUP = 16
S5_GROUPS = S5_WIDTH // S5_CH_PER_GROUP
S5_STATE = 64
DN_HEADS = 8
DN_HEAD_DIM = GROUP_W // DN_HEADS
DN_CHUNK = 64
GQA_HEADS = 8
GQA_KV_HEADS = 2
GQA_HEAD_DIM = GROUP_W // GQA_HEADS
DIFF_HEADS = 8
DIFF_V_DIM = GROUP_W // DIFF_HEADS
DIFF_QK_DIM = DIFF_V_DIM // 2
Q_BLOCK = 128
ROPE_THETA = 10000.0
PEER_HEADS = 8
PEER_KEY_DIM = 128
PEER_N_KEYS = 128
PEER_TOPK = 16
N_MOD = 6
EPS = 1e-6

_REF_SPLITS = (('s5_u', 1024), ('dn_q', 1024), ('dn_k', 1024), ('dn_v', 1024), ('dn_gate', 1024),
               ('dn_a', 16), ('dn_b', 16), ('gq', 1024), ('gk', 256), ('gv', 256),
               ('fq', 1024), ('fk', 1024), ('fv', 1024))
_Z_ORDER = ('s5_u', 'dn_q', 'dn_k', 'dn_v', 'dn_gate', 'gq', 'fq', 'fk', 'fv', 'gk', 'gv', 'dn_a', 'dn_b')
LANES = 128
IN_PROJ_TN = 1024
VMEM_LIMIT = 56 * 1024 * 1024


def _z_layout():
    widths = dict(_REF_SPLITS)
    ref_start, s = {}, 0
    for name, w in _REF_SPLITS:
        ref_start[name] = s
        s += w
    z_start, s = {}, 0
    for name in _Z_ORDER:
        z_start[name] = s
        s += widths[name]
    total = -(-s // IN_PROJ_TN) * IN_PROJ_TN
    return widths, ref_start, z_start, total


Z_WIDTH, Z_REF_START, Z_START, Z_COLS = _z_layout()


def _cparams(sem):
    return pltpu.CompilerParams(dimension_semantics=sem, vmem_limit_bytes=VMEM_LIMIT)


def _silu(x):
    return x * (1.0 / (1.0 + jnp.exp(-x)))


def _gelu_tanh(x):
    return 0.5 * x * (1.0 + jnp.tanh(math.sqrt(2.0 / math.pi) * (x + 0.044715 * (x * x * x))))


def _mod_kernel(cond_ref, w_ref, b_ref, o_ref):
    a = _silu(cond_ref[...]).astype(BF16)
    o_ref[0] = jnp.dot(a, w_ref[0].astype(BF16), preferred_element_type=F32) + b_ref[0]


def modulation(cond, w_mod, b_mod, tn=512):
    depth, d, n = w_mod.shape
    return pl.pallas_call(
        _mod_kernel,
        out_shape=jax.ShapeDtypeStruct((depth, 8, n), F32),
        grid=(depth, n // tn),
        in_specs=[pl.BlockSpec((8, d), lambda l, j: (0, 0)),
                  pl.BlockSpec((1, d, tn), lambda l, j: (l, 0, j)),
                  pl.BlockSpec((1, 1, tn), lambda l, j: (l, 0, j))],
        out_specs=pl.BlockSpec((1, 8, tn), lambda l, j: (l, 0, j)),
        compiler_params=_cparams(("arbitrary", "arbitrary")),
        name="modulation",
    )(cond, w_mod, b_mod.reshape(depth, 1, n))


def _mod_row(i, tile, n_ctx_tok, lat_len):
    t0 = i * tile
    return jnp.where(t0 < n_ctx_tok, 0, 1 + (t0 - n_ctx_tok) // lat_len)


def _norm_mod(x, g, shift, scale):
    r = lax.rsqrt(jnp.mean(x * x, axis=-1, keepdims=True) + EPS)
    return (x * r * g) * (1.0 + scale) + shift


def _inproj_kernel(x_ref, g_ref, mod_ref, w_ref, o_ref, h_ref):
    @pl.when(pl.program_id(1) == 0)
    def _():
        m = mod_ref[0]
        h_ref[...] = _norm_mod(x_ref[...], g_ref[...], m[0:1], m[1:2]).astype(BF16)

    o_ref[...] = jnp.dot(h_ref[...], w_ref[...], preferred_element_type=F32).astype(o_ref.dtype)


def in_projection(x, g, mod, w, n_ctx_tok, lat_len, tm=512, tn=IN_PROJ_TN):
    t, d = x.shape
    n = w.shape[1]
    row = functools.partial(_mod_row, tile=tm, n_ctx_tok=n_ctx_tok, lat_len=lat_len)
    return pl.pallas_call(
        _inproj_kernel,
        out_shape=jax.ShapeDtypeStruct((t, n), F32),
        grid=(t // tm, n // tn),
        in_specs=[pl.BlockSpec((tm, d), lambda i, j: (i, 0)),
                  pl.BlockSpec((1, d), lambda i, j: (0, 0)),
                  pl.BlockSpec((1, N_MOD, d), lambda i, j: (row(i), 0, 0)),
                  pl.BlockSpec((d, tn), lambda i, j: (0, j))],
        out_specs=pl.BlockSpec((tm, tn), lambda i, j: (i, j)),
        scratch_shapes=[pltpu.VMEM((tm, d), BF16)],
        compiler_params=_cparams(("arbitrary", "arbitrary")),
        name="in_projection",
    )(x, g.reshape(1, d), mod, w)


def _outproj_kernel(ya_ref, yb_ref, ycc_ref, ycl_ref, ydc_ref, ydl_ref, w_ref, x_ref, mod_ref, o_ref, *, nc):
    ys = (ya_ref[...], yb_ref[...], _region_pick(ycc_ref, ycl_ref, nc), _region_pick(ydc_ref, ydl_ref, nc))
    acc = None
    for m, y in enumerate(ys):
        part = jnp.dot(y, w_ref[m * GROUP_W:(m + 1) * GROUP_W, :], preferred_element_type=F32)
        acc = part if acc is None else acc + part
    o_ref[...] = x_ref[...] + mod_ref[0][2:3] * acc


def out_projection(y_a, y_b, y_c, y_d, w, x, mod, n_ctx_tok, lat_len, tm=512, tn=1024):
    t, n = x.shape
    kdim = w.shape[0]
    row = functools.partial(_mod_row, tile=tm, n_ctx_tok=n_ctx_tok, lat_len=lat_len)
    y_spec = pl.BlockSpec((tm, GROUP_W), lambda i, j: (i, 0))
    pair = _region_specs(tm, GROUP_W, n_ctx_tok)
    return pl.pallas_call(
        functools.partial(_outproj_kernel, nc=n_ctx_tok // tm),
        out_shape=jax.ShapeDtypeStruct((t, n), F32),
        grid=(t // tm, n // tn),
        in_specs=[y_spec, y_spec, *pair, *pair,
                  pl.BlockSpec((kdim, tn), lambda i, j: (0, j)),
                  pl.BlockSpec((tm, tn), lambda i, j: (i, j)),
                  pl.BlockSpec((1, N_MOD, tn), lambda i, j: (row(i), 0, j))],
        out_specs=pl.BlockSpec((tm, tn), lambda i, j: (i, j)),
        compiler_params=_cparams(("arbitrary", "arbitrary")),
        name="out_projection",
    )(y_a, y_b, *y_c, *y_d, w, x, mod)


def _top_rows(x, k):
    vals = []
    cur = x
    for _ in range(k):
        mx = jnp.max(cur, axis=0, keepdims=True)
        vals.append(mx)
        cur = jnp.where(cur == mx, -jnp.inf, cur)
    return jnp.concatenate(vals, axis=0)


def _pair_sums(a, b):
    k, cols = a.shape
    sub = 8
    pad_rows = -(-k // sub) * sub - k
    neg = jnp.full((pad_rows, cols), -jnp.inf, F32)
    a_pad = jnp.concatenate([a, neg], axis=0)
    b_pad = jnp.concatenate([b, neg], axis=0)
    pieces = [a[0:1] + b_pad]
    row = lax.broadcasted_iota(jnp.int32, (sub, cols), 0)
    for i in range(1, sub):
        pieces.append(jnp.where(row < k // (i + 1), a[i:i + 1] + b_pad[0:sub], -jnp.inf))
    pieces.append(a_pad[sub:] + b[0:1])
    return jnp.concatenate(pieces, axis=0)


def _peer_query_kernel(x_ref, g_ref, mod_ref, wq_ref, keys_ref,
                       h2t_ref, s2_ref, e2_ref, thr_ref, e1_ref):
    m = mod_ref[0]
    h = _norm_mod(x_ref[...], g_ref[...], m[3:4], m[4:5])
    hb = h.astype(BF16)
    h2t_ref[...] = h.T.astype(BF16)
    q = jnp.dot(hb, wq_ref[...], preferred_element_type=F32)
    nt = (((1,), (1,)), ((), ()))
    for hd in range(PEER_HEADS):
        qh = q[:, hd * PEER_KEY_DIM:(hd + 1) * PEER_KEY_DIM].astype(BF16)
        s1 = lax.dot_general(keys_ref[2 * hd], qh, nt, preferred_element_type=F32)
        s2 = lax.dot_general(keys_ref[2 * hd + 1], qh, nt, preferred_element_type=F32)
        a = _top_rows(s1, PEER_TOPK + 1)
        b = _top_rows(s2, PEER_TOPK + 1)
        v = _top_rows(_pair_sums(a, b), PEER_TOPK + 1)
        z = jnp.sum(jnp.exp(v[:PEER_TOPK] - v[0:1]), axis=0, keepdims=True)
        tau = 0.5 * (v[PEER_TOPK - 1:PEER_TOPK] + v[PEER_TOPK:PEER_TOPK + 1])
        s2_ref[hd] = s2
        e2_ref[hd] = jnp.exp(s2 - b[0:1])
        thr_ref[hd] = tau - s1
        e1_ref[hd] = jnp.exp(s1 - a[0:1]) / z


def peer_query(x, g, mod, wq, keys_pad, n_ctx_tok, lat_len, tq=256):
    t, d = x.shape
    row = functools.partial(_mod_row, tile=tq, n_ctx_tok=n_ctx_tok, lat_len=lat_len)
    aux = jax.ShapeDtypeStruct((PEER_HEADS, PEER_N_KEYS, t), F32)
    aux_spec = pl.BlockSpec((PEER_HEADS, PEER_N_KEYS, tq), lambda i: (0, 0, i))
    return pl.pallas_call(
        _peer_query_kernel,
        out_shape=(jax.ShapeDtypeStruct((d, t), BF16), aux, aux, aux, aux),
        grid=(t // tq,),
        in_specs=[pl.BlockSpec((tq, d), lambda i: (i, 0)),
                  pl.BlockSpec((1, d), lambda i: (0, 0)),
                  pl.BlockSpec((1, N_MOD, d), lambda i: (row(i), 0, 0)),
                  pl.BlockSpec(wq.shape, lambda i: (0, 0)),
                  pl.BlockSpec(keys_pad.shape, lambda i: (0, 0, 0))],
        out_specs=(pl.BlockSpec((d, tq), lambda i: (0, i)), aux_spec, aux_spec, aux_spec, aux_spec),
        compiler_params=_cparams(("arbitrary",)),
        name="peer_query",
    )(x, g.reshape(1, d), mod, wq, keys_pad)


def _peer_main_kernel(xt_ref, u_ref, vt_ref, s2_ref, e2_ref, thr_ref, e1_ref, o_ref, *, n_sub):
    k = pl.program_id(1)

    @pl.when(k == 0)
    def _():
        o_ref[...] = jnp.zeros_like(o_ref)

    ws = []
    for r in range(n_sub):
        i1 = k * n_sub + r
        w = None
        for hd in range(PEER_HEADS):
            thr = thr_ref[hd, pl.ds(i1, 1), :]
            e1 = e1_ref[hd, pl.ds(i1, 1), :]
            term = jnp.where(s2_ref[hd] >= thr, e2_ref[hd], 0.0) * e1
            w = term if w is None else w + term
        ws.append(w)
    w = jnp.concatenate(ws, axis=0) if n_sub > 1 else ws[0]

    hid = jnp.dot(u_ref[...], xt_ref[...], preferred_element_type=F32)
    hw = (_gelu_tanh(hid) * w).astype(BF16)
    o_ref[...] += jnp.dot(vt_ref[...], hw, preferred_element_type=F32)


def peer_main(h2t, u, vt, s2, e2, thr, e1, tt=512, te=512):
    d, t = h2t.shape
    n_exp = u.shape[0]
    n_sub = te // PEER_N_KEYS
    once = pl.Buffered(1)
    aux_spec = pl.BlockSpec((PEER_HEADS, PEER_N_KEYS, tt), lambda i, k: (0, 0, i), pipeline_mode=once)
    return pl.pallas_call(
        functools.partial(_peer_main_kernel, n_sub=n_sub),
        out_shape=jax.ShapeDtypeStruct((d, t), F32),
        grid=(t // tt, n_exp // te),
        in_specs=[pl.BlockSpec((d, tt), lambda i, k: (0, i), pipeline_mode=once),
                  pl.BlockSpec((te, d), lambda i, k: (k, 0)),
                  pl.BlockSpec((d, te), lambda i, k: (0, k)),
                  aux_spec, aux_spec, aux_spec, aux_spec],
        out_specs=pl.BlockSpec((d, tt), lambda i, k: (0, i), pipeline_mode=once),
        compiler_params=_cparams(("arbitrary", "arbitrary")),
        name="peer_main",
    )(h2t, u, vt, s2, e2, thr, e1)


def _peer_residual_kernel(x_ref, pt_ref, mod_ref, g_ref, o_ref, *, final):
    x = x_ref[...] + mod_ref[0][5:6] * pt_ref[...].T
    if final:
        r = lax.rsqrt(jnp.mean(x * x, axis=-1, keepdims=True) + EPS)
        x = x * r * g_ref[...]
    o_ref[...] = x


def peer_residual(x, peer_t, mod, g_final, n_ctx_tok, lat_len, final, tok0, n_tok, tr=256):
    d = x.shape[1]
    first = tok0 // tr
    mod_row = functools.partial(_mod_row, tile=tr, n_ctx_tok=n_ctx_tok, lat_len=lat_len)
    row = lambda i: mod_row(first + i)
    return pl.pallas_call(
        functools.partial(_peer_residual_kernel, final=final),
        out_shape=jax.ShapeDtypeStruct((n_tok, d), F32),
        grid=(n_tok // tr,),
        in_specs=[pl.BlockSpec((tr, d), lambda i: (first + i, 0)),
                  pl.BlockSpec((d, tr), lambda i: (0, first + i)),
                  pl.BlockSpec((1, N_MOD, d), lambda i: (row(i), 0, 0)),
                  pl.BlockSpec((1, d), lambda i: (0, 0))],
        out_specs=pl.BlockSpec((tr, d), lambda i: (i, 0)),
        compiler_params=_cparams(("arbitrary",)),
        name="peer_residual",
    )(x, peer_t, mod, g_final.reshape(1, d))


_NT = (((1,), (1,)), ((), ()))


def _head_rmsnorm(x, g):
    return x * lax.rsqrt(jnp.mean(x * x, axis=-1, keepdims=True) + EPS) * g


def _rope_half(x, cosf, sinf):
    return x * cosf + pltpu.roll(x, GQA_HEAD_DIM // 2, 1) * sinf


def _rope_quarter(x, cosf, sinf):
    lane = lax.broadcasted_iota(jnp.int32, x.shape, 1)
    half = DIFF_QK_DIM // 2
    swapped = jnp.where((lane % DIFF_QK_DIM) < half,
                        pltpu.roll(x, LANES - half, 1), pltpu.roll(x, half, 1))
    return x * cosf + swapped * sinf


def _key_prep_kernel(gk_ref, fk_ref, kn_ref, cg_ref, sg_ref, cd_ref, sd_ref, kg_ref, kd_ref):
    for hh in range(GQA_KV_HEADS):
        sl = slice(hh * GQA_HEAD_DIM, (hh + 1) * GQA_HEAD_DIM)
        kg_ref[:, sl] = _rope_half(_head_rmsnorm(gk_ref[:, sl], kn_ref[...]), cg_ref[...], sg_ref[...])
    for hh in range(DIFF_HEADS):
        sl = slice(hh * LANES, (hh + 1) * LANES)
        kd_ref[:, sl] = _rope_quarter(fk_ref[:, sl], cd_ref[...], sd_ref[...])


def key_prep(z, k_norm, tabs, tm=256):
    t = z.shape[0]
    gk_blk = Z_START['gk'] // Z_WIDTH['gk']
    fk_blk = Z_START['fk'] // Z_WIDTH['fk']
    tab_spec = pl.BlockSpec((tm, LANES), lambda i: (i, 0))
    return pl.pallas_call(
        _key_prep_kernel,
        out_shape=(jax.ShapeDtypeStruct((t, Z_WIDTH['gk']), F32), jax.ShapeDtypeStruct((t, Z_WIDTH['fk']), F32)),
        grid=(t // tm,),
        in_specs=[pl.BlockSpec((tm, Z_WIDTH['gk']), lambda i: (i, gk_blk)),
                  pl.BlockSpec((tm, Z_WIDTH['fk']), lambda i: (i, fk_blk)),
                  pl.BlockSpec((1, GQA_HEAD_DIM), lambda i: (0, 0)),
                  tab_spec, tab_spec, tab_spec, tab_spec],
        out_specs=(pl.BlockSpec((tm, Z_WIDTH['gk']), lambda i: (i, 0)),
                   pl.BlockSpec((tm, Z_WIDTH['fk']), lambda i: (i, 0))),
        compiler_params=_cparams(("arbitrary",)),
        name="key_prep",
    )(z, z, k_norm.reshape(1, -1), *tabs)


def _softmax_rows(s):
    m = jnp.max(s, axis=-1, keepdims=True)
    p = jnp.exp(s - m)
    return p, jnp.sum(p, axis=-1, keepdims=True)


def _gqa_kernel(q_ref, qn_ref, cos_ref, sin_ref, k_ref, v_ref, o_ref):
    tq = q_ref.shape[0]
    n_rep = GQA_HEADS // GQA_KV_HEADS
    qs = []
    for hh in range(n_rep):
        x = _head_rmsnorm(q_ref[:, hh * GQA_HEAD_DIM:(hh + 1) * GQA_HEAD_DIM], qn_ref[...])
        x = _rope_half(x, cos_ref[...], sin_ref[...])
        qs.append((x * (GQA_HEAD_DIM ** -0.5)).astype(BF16))
    q = jnp.concatenate(qs, axis=0)
    s = lax.dot_general(q, k_ref[0], _NT, preferred_element_type=F32)
    p, l = _softmax_rows(s)
    o = jnp.dot(p.astype(BF16), v_ref[0], preferred_element_type=F32) / l
    o_ref[...] = jnp.concatenate([o[hh * tq:(hh + 1) * tq] for hh in range(n_rep)], axis=1).astype(o_ref.dtype)


def gqa_attention(z, q_norm, cos, sin, keys, vals, tok0, seq_len, tq):
    b, lk, _ = keys.shape
    n_rep = GQA_HEADS // GQA_KV_HEADS
    qw = n_rep * GQA_HEAD_DIM
    q_blk = Z_START['gq'] // qw
    nq = seq_len // tq
    row = lambda bi, g, qi: (tok0 + bi * seq_len) // tq + qi
    kv_spec = pl.BlockSpec((1, lk, GQA_HEAD_DIM), lambda bi, g, qi: (bi, 0, g))
    tab_spec = pl.BlockSpec((tq, LANES), lambda bi, g, qi: (row(bi, g, qi), 0))
    return pl.pallas_call(
        _gqa_kernel,
        out_shape=jax.ShapeDtypeStruct((b * seq_len, GROUP_W), BF16),
        grid=(b, GQA_KV_HEADS, nq),
        in_specs=[pl.BlockSpec((tq, qw), lambda bi, g, qi: (row(bi, g, qi), q_blk + g)),
                  pl.BlockSpec((1, GQA_HEAD_DIM), lambda bi, g, qi: (0, 0)),
                  tab_spec, tab_spec, kv_spec, kv_spec],
        out_specs=pl.BlockSpec((tq, qw), lambda bi, g, qi: (bi * nq + qi, g)),
        compiler_params=_cparams(("arbitrary", "arbitrary", "arbitrary")),
        name="gqa_attention",
    )(z, q_norm.reshape(1, -1), cos, sin, keys, vals)


def _diff_kernel(q_ref, cos_ref, sin_ref, k_ref, v_ref, lp_ref, g_ref, o_ref, *, lam_init):
    tq = q_ref.shape[0]
    lp = lp_ref[...]
    lam = (jnp.exp(jnp.sum(lp[0:1] * lp[1:2], axis=1, keepdims=True))
           - jnp.exp(jnp.sum(lp[2:3] * lp[3:4], axis=1, keepdims=True)) + lam_init)
    q = _rope_quarter(q_ref[...], cos_ref[...], sin_ref[...]) * (DIFF_QK_DIM ** -0.5)
    lane = lax.broadcasted_iota(jnp.int32, q.shape, 1)
    q12 = jnp.concatenate([jnp.where(lane < DIFF_QK_DIM, q, 0.0), jnp.where(lane >= DIFF_QK_DIM, q, 0.0)], axis=0)
    s = lax.dot_general(q12.astype(BF16), k_ref[0], _NT, preferred_element_type=F32)
    p, l = _softmax_rows(s)
    p = p / l
    w = p[:tq] - lam * p[tq:]
    o = jnp.dot(w.astype(BF16), v_ref[0], preferred_element_type=F32)
    o_ref[...] = (_head_rmsnorm(o, g_ref[...]) * (1.0 - lam_init)).astype(o_ref.dtype)


def diff_attention(z, cos, sin, keys, vals, lam_params, subln_g, layer_idx, tok0, seq_len, tq):
    b, lk, _ = keys.shape
    q_blk = Z_START['fq'] // LANES
    nq = seq_len // tq
    lam_init = 0.8 - 0.6 * math.exp(-0.3 * layer_idx)
    row = lambda bi, h, qi: (tok0 + bi * seq_len) // tq + qi
    kv_spec = pl.BlockSpec((1, lk, LANES), lambda bi, h, qi: (bi, 0, h))
    tab_spec = pl.BlockSpec((tq, LANES), lambda bi, h, qi: (row(bi, h, qi), 0))
    return pl.pallas_call(
        functools.partial(_diff_kernel, lam_init=lam_init),
        out_shape=jax.ShapeDtypeStruct((b * seq_len, GROUP_W), BF16),
        grid=(b, DIFF_HEADS, nq),
        in_specs=[pl.BlockSpec((tq, LANES), lambda bi, h, qi: (row(bi, h, qi), q_blk + h)),
                  tab_spec, tab_spec, kv_spec, kv_spec,
                  pl.BlockSpec(lam_params.shape, lambda bi, h, qi: (0, 0)),
                  pl.BlockSpec((1, DIFF_V_DIM), lambda bi, h, qi: (0, 0))],
        out_specs=pl.BlockSpec((tq, LANES), lambda bi, h, qi: (bi * nq + qi, h)),
        compiler_params=_cparams(("arbitrary", "arbitrary", "arbitrary")),
        name="diff_attention",
    )(z, cos, sin, keys, vals, lam_params, subln_g.reshape(1, -1))


def _rope_lane_tables(length, dim, n_ctx_tok, dec_batch):
    cos, sin = _rope_tables(length, dim)
    reps = LANES // dim
    cosf = jnp.tile(jnp.concatenate([cos, cos], axis=1), (dec_batch, reps))
    sinf = jnp.tile(jnp.concatenate([-sin, sin], axis=1), (dec_batch, reps))
    ones = jnp.ones((n_ctx_tok, LANES), F32)
    return jnp.concatenate([ones, cosf], axis=0), jnp.concatenate([jnp.zeros_like(ones), sinf], axis=0)


S5_BLK_GROUPS = LANES // S5_CH_PER_GROUP
S5_BLOCKS = S5_GROUPS // S5_BLK_GROUPS
S5_BLK_STATE = S5_BLK_GROUPS * S5_STATE
SUBLANES = 8


def _s5_tables(lam_re, lam_im, log_dt, b_re, b_im, c_re, c_im):
    dt = jnp.exp(log_dt)[..., None]
    mag = jnp.exp(lam_re * dt)
    ab_re, ab_im = mag * jnp.cos(lam_im * dt), mag * jnp.sin(lam_im * dt)
    den = lam_re * lam_re + lam_im * lam_im
    nr = ab_re - 1.0
    coef_re = (nr * lam_re + ab_im * lam_im) / den
    coef_im = (ab_im * lam_re - nr * lam_im) / den
    bb_re = coef_re[..., None] * b_re - coef_im[..., None] * b_im
    bb_im = coef_re[..., None] * b_im + coef_im[..., None] * b_re
    j, gl, p, h = S5_BLOCKS, S5_BLK_GROUPS, S5_STATE, S5_CH_PER_GROUP
    eye = jnp.eye(gl, dtype=F32)

    def in_blocks(bb):
        t = bb.reshape(2, j, gl, p, h)
        return jnp.einsum('djgph,gk->djghkp', t, eye).reshape(2, j, gl * h, gl * p)

    def out_blocks(cc):
        t = cc.reshape(2, j, gl, h, p)
        return jnp.einsum('djghp,gk->djgpkh', t, eye).reshape(2, j, gl * p, gl * h)

    wb = jnp.concatenate([in_blocks(bb_re), in_blocks(bb_im)], axis=-1).astype(BF16)
    wc = jnp.concatenate([out_blocks(c_re), out_blocks(-c_im)], axis=-2).astype(BF16)

    ar, ai = ab_re.reshape(2, j, 1, gl * p), ab_im.reshape(2, j, 1, gl * p)
    pows = [(ar, ai)]
    for _ in range(SUBLANES - 1):
        pr, pi = pows[-1]
        pows.append((pr * ar - pi * ai, pr * ai + pi * ar))
    r = jnp.arange(SUBLANES).reshape(1, 1, SUBLANES, 1)
    rows = []
    for d in range(2):
        dr = []
        for s in (1, 2, 4):
            mask = (r >= s) if d == 0 else (r <= SUBLANES - 1 - s)
            dr += [jnp.where(mask, pows[s - 1][0][d:d + 1], 0.0), jnp.where(mask, pows[s - 1][1][d:d + 1], 0.0)]
        order = range(SUBLANES) if d == 0 else range(SUBLANES - 1, -1, -1)
        dr.append(jnp.concatenate([pows[k][0][d:d + 1] for k in order], axis=2))
        dr.append(jnp.concatenate([pows[k][1][d:d + 1] for k in order], axis=2))
        rows.append(jnp.stack([jnp.broadcast_to(x, (1, j, SUBLANES, gl * p)) for x in dr], axis=2))
    return wb, wc, jnp.concatenate(rows, axis=0)


def _s5_scan_tile(xr, xi, coef_ref, d, hr, hi):
    for k, s in enumerate((1, 2, 4)):
        shift = s if d == 0 else SUBLANES - s
        ar, ai = coef_ref[d, 0, 2 * k], coef_ref[d, 0, 2 * k + 1]
        sr, si = pltpu.roll(xr, shift, 0), pltpu.roll(xi, shift, 0)
        xr, xi = xr + ar * sr - ai * si, xi + ar * si + ai * sr
    cr, ci = coef_ref[d, 0, 6], coef_ref[d, 0, 7]
    return xr + cr * hr - ci * hi, xi + cr * hi + ci * hr


def _s5_scan_kernel(uf_ref, ub_ref, wb_ref, wc_ref, coef_ref, h0_ref, yf_ref, yb_ref, hfin_ref,
                    sr_ref, si_ref, carry_ref):
    c = pl.program_id(2)
    bs = S5_BLK_STATE
    tc = uf_ref.shape[0]
    n_tiles = tc // SUBLANES

    @pl.when(c == 0)
    def _():
        carry_ref[...] = h0_ref[0, :, 0]

    for d, (u_ref, y_ref) in enumerate(((uf_ref, yf_ref), (ub_ref, yb_ref))):
        bu = jnp.dot(u_ref[...].astype(BF16), wb_ref[d, 0], preferred_element_type=F32)
        sr_ref[...] = bu[:, :bs]
        si_ref[...] = bu[:, bs:]

        def body(i, carry, d=d):
            hr, hi = carry
            tile = i if d == 0 else n_tiles - 1 - i
            rows = pl.ds(pl.multiple_of(tile * SUBLANES, SUBLANES), SUBLANES)
            xr, xi = _s5_scan_tile(sr_ref[rows, :], si_ref[rows, :], coef_ref, d, hr, hi)
            sr_ref[rows, :] = xr
            si_ref[rows, :] = xi
            last = SUBLANES - 1 if d == 0 else 0
            return xr[last:last + 1], xi[last:last + 1]

        h0 = carry_ref[d]
        hr, hi = lax.fori_loop(0, n_tiles, body, (h0[:, :bs], h0[:, bs:]))
        carry_ref[d] = jnp.concatenate([hr, hi], axis=1)
        hcat = jnp.concatenate([sr_ref[...], si_ref[...]], axis=1).astype(BF16)
        y_ref[...] = jnp.dot(hcat, wc_ref[d, 0], preferred_element_type=F32)

    hfin_ref[0, :, 0] = carry_ref[...]


def s5_scan(z, wb, wc, coef, h0, tok0, seq_len, tc):
    b = h0.shape[0]
    nt = seq_len // tc
    width = 2 * S5_BLK_STATE
    u_blk = Z_START['s5_u'] // LANES
    row_f = lambda bi, j, c: ((tok0 + bi * seq_len) // tc + c, u_blk + j)
    row_b = lambda bi, j, c: ((tok0 + bi * seq_len) // tc + nt - 1 - c, u_blk + j)
    st_spec = pl.BlockSpec((1, 2, 1, 1, width), lambda bi, j, c: (bi, 0, j, 0, 0))
    y_shape = jax.ShapeDtypeStruct((b * seq_len, S5_WIDTH), F32)
    return pl.pallas_call(
        _s5_scan_kernel,
        out_shape=(y_shape, y_shape, jax.ShapeDtypeStruct(h0.shape, F32)),
        grid=(b, S5_BLOCKS, nt),
        in_specs=[pl.BlockSpec((tc, LANES), row_f),
                  pl.BlockSpec((tc, LANES), row_b),
                  pl.BlockSpec((2, 1, LANES, width), lambda bi, j, c: (0, j, 0, 0)),
                  pl.BlockSpec((2, 1, width, LANES), lambda bi, j, c: (0, j, 0, 0)),
                  pl.BlockSpec((2, 1, SUBLANES, SUBLANES, S5_BLK_STATE), lambda bi, j, c: (0, j, 0, 0, 0)),
                  st_spec],
        out_specs=(pl.BlockSpec((tc, LANES), lambda bi, j, c: (bi * nt + c, j)),
                   pl.BlockSpec((tc, LANES), lambda bi, j, c: (bi * nt + nt - 1 - c, j)),
                   st_spec),
        scratch_shapes=[pltpu.VMEM((tc, S5_BLK_STATE), F32), pltpu.VMEM((tc, S5_BLK_STATE), F32),
                        pltpu.VMEM((2, 1, width), F32)],
        compiler_params=_cparams(("arbitrary", "arbitrary", "arbitrary")),
        name="s5_scan",
    )(z, z, wb, wc, coef, h0)


def _region_specs(tm, width, n_ctx_tok):
    nc = n_ctx_tok // tm
    return (pl.BlockSpec((tm, width), lambda i, *_: (jnp.minimum(i, nc - 1), 0)),
            pl.BlockSpec((tm, width), lambda i, *_: (jnp.maximum(i - nc, 0), 0)))


def _region_pick(ctx_ref, lat_ref, n_ctx_tiles):
    return jnp.where(pl.program_id(0) < n_ctx_tiles, ctx_ref[...], lat_ref[...])


def _s5_glu_kernel(yfc_ref, yfl_ref, ybc_ref, ybl_ref, u_ref, d_ref, w_ref, o_ref, *, nc):
    y = _gelu_tanh(_region_pick(yfc_ref, yfl_ref, nc) + _region_pick(ybc_ref, ybl_ref, nc)
                   + d_ref[...] * u_ref[...])
    gate = jnp.dot(y.astype(BF16), w_ref[...], preferred_element_type=F32)
    o_ref[...] = (y * (1.0 / (1.0 + jnp.exp(-gate)))).astype(o_ref.dtype)


def s5_glu(yf, yb, z, d_skip, w_glu, n_ctx_tok, tm=512):
    t = z.shape[0]
    blk = pl.BlockSpec((tm, S5_WIDTH), lambda i: (i, 0))
    pair = _region_specs(tm, S5_WIDTH, n_ctx_tok)
    return pl.pallas_call(
        functools.partial(_s5_glu_kernel, nc=n_ctx_tok // tm),
        out_shape=jax.ShapeDtypeStruct((t, S5_WIDTH), BF16),
        grid=(t // tm,),
        in_specs=[*pair, *pair, pl.BlockSpec((tm, S5_WIDTH), lambda i: (i, Z_START['s5_u'] // S5_WIDTH)),
                  pl.BlockSpec((1, S5_WIDTH), lambda i: (0, 0)),
                  pl.BlockSpec((S5_WIDTH, S5_WIDTH), lambda i: (0, 0))],
        out_specs=blk,
        compiler_params=_cparams(("arbitrary",)),
        name="s5_glu",
    )(*yf, *yb, z, d_skip.reshape(1, -1), w_glu)


def _s5_state_in(re, im):
    b = re.shape[0]
    r = re.reshape(b, 2, S5_BLOCKS, 1, S5_BLK_STATE)
    i = im.reshape(b, 2, S5_BLOCKS, 1, S5_BLK_STATE)
    return jnp.concatenate([r, i], axis=-1)


def _s5_state_out(h):
    b = h.shape[0]
    re = h[..., :S5_BLK_STATE].reshape(b, 2, S5_GROUPS, S5_STATE)
    im = h[..., S5_BLK_STATE:].reshape(b, 2, S5_GROUPS, S5_STATE)
    return re, im


def _dn_conv_kernel(x_ref, prev_ref, next_ref, w_ref, o_ref, *, n_ctx_tok, seq, lat_len):
    i = pl.program_id(0)
    cg = pl.program_id(1)
    tm = x_ref.shape[0]
    t0 = i * tm
    in_ctx = t0 < n_ctx_tok
    pos = jnp.where(in_ctx, t0 % seq, (t0 - n_ctx_tok) % lat_len)
    length = jnp.where(in_ctx, seq, lat_len)
    x = x_ref[...]
    row = lax.broadcasted_iota(jnp.int32, x.shape, 0)
    before = jnp.where(pos > 0, prev_ref[SUBLANES - 1:SUBLANES, :], 0.0)
    after = jnp.where(pos + tm < length, next_ref[0:1, :], 0.0)
    xp = jnp.where(row == 0, before, pltpu.roll(x, 1, 0))
    xn = jnp.where(row == tm - 1, after, pltpu.roll(x, tm - 1, 0))
    y = _silu(w_ref[0:1, :] * xp + w_ref[1:2, :] * x + w_ref[2:3, :] * xn)
    scale = jnp.where(cg == 0, DN_HEAD_DIM ** -0.5, 1.0)
    for hh in range(DN_HEADS):
        sl = slice(hh * DN_HEAD_DIM, (hh + 1) * DN_HEAD_DIM)
        yh = y[:, sl]
        nrm = yh * (lax.rsqrt(jnp.sum(yh * yh, axis=-1, keepdims=True) + EPS) * scale)
        o_ref[:, sl] = jnp.where(cg == 2, yh, nrm)


def dn_conv(z, conv_w, n_ctx_tok, seq, lat_len, tm=256):
    t = z.shape[0]
    first = Z_START['dn_q'] // GROUP_W
    nsub = tm // SUBLANES
    last_sub = t // SUBLANES - 1
    return pl.pallas_call(
        functools.partial(_dn_conv_kernel, n_ctx_tok=n_ctx_tok, seq=seq, lat_len=lat_len),
        out_shape=jax.ShapeDtypeStruct((t, 3 * GROUP_W), F32),
        grid=(t // tm, 3),
        in_specs=[pl.BlockSpec((tm, GROUP_W), lambda i, cg: (i, first + cg)),
                  pl.BlockSpec((SUBLANES, GROUP_W), lambda i, cg: (jnp.maximum(i * nsub - 1, 0), first + cg)),
                  pl.BlockSpec((SUBLANES, GROUP_W), lambda i, cg: (jnp.minimum((i + 1) * nsub, last_sub), first + cg)),
                  pl.BlockSpec((3, GROUP_W), lambda i, cg: (0, cg))],
        out_specs=pl.BlockSpec((tm, GROUP_W), lambda i, cg: (i, cg)),
        compiler_params=_cparams(("arbitrary", "arbitrary")),
        name="dn_conv",
    )(z, z, z, conv_w)


def _dn_gate_kernel(x_ref, neg_a_ref, dtb_ref, o_ref):
    xt = x_ref[...].T
    nh = 2 * DN_HEADS
    a = xt[0:nh] + dtb_ref[...]
    softplus = jnp.maximum(a, 0.0) + jnp.log(1.0 + jnp.exp(-jnp.abs(a)))
    o_ref[0:nh, :] = neg_a_ref[...] * softplus
    o_ref[nh:2 * nh, :] = 1.0 / (1.0 + jnp.exp(-xt[nh:2 * nh]))


def dn_gates(z, a_log, dt_bias, tm=256):
    t = z.shape[0]
    nh = 2 * DN_HEADS
    neg_a = jnp.broadcast_to(-jnp.exp(a_log).reshape(nh, 1), (nh, tm))
    dtb = jnp.broadcast_to(dt_bias.reshape(nh, 1), (nh, tm))
    return pl.pallas_call(
        _dn_gate_kernel,
        out_shape=jax.ShapeDtypeStruct((2 * nh, t), F32),
        grid=(t // tm,),
        in_specs=[pl.BlockSpec((tm, LANES), lambda i: (i, Z_START['dn_a'] // LANES)),
                  pl.BlockSpec((nh, tm), lambda i: (0, 0)),
                  pl.BlockSpec((nh, tm), lambda i: (0, 0))],
        out_specs=pl.BlockSpec((2 * nh, tm), lambda i: (0, i)),
        compiler_params=_cparams(("arbitrary",)),
        name="dn_gates",
    )(z, neg_a, dtb)


def _dn_masks(c, upper):
    rows = lax.broadcasted_iota(jnp.int32, (c, c), 0)
    cols = lax.broadcasted_iota(jnp.int32, (c, c), 1)
    levels = []
    s = 1
    while s < c:
        same = (rows ^ cols) < 2 * s
        lo, hi = (rows & s) != 0, (cols & s) == 0
        if upper:
            lo, hi = (rows & s) == 0, (cols & s) != 0
        levels.append(jnp.where(same, jnp.where(lo, jnp.where(hi, 1.0, 0.0), 0.0), 0.0))
        s *= 2
    return dict(eye=rows == cols,
                causal=(rows <= cols) if upper else (rows >= cols),
                causal_t=(rows >= cols) if upper else (rows <= cols),
                strict=(rows < cols) if upper else (rows > cols),
                levels=levels)


def _mm(a, b):
    return jnp.dot(a.astype(BF16), b.astype(BF16), preferred_element_type=F32)


def _mm_nt(a, b):
    return lax.dot_general(a.astype(BF16), b.astype(BF16), _NT, preferred_element_type=F32)


def _dn_chunk_local(q, k, v, g_row, beta_row, masks):
    eye, causal, causal_t = masks['eye'], masks['causal'], masks['causal_t']
    g_col = jnp.sum(jnp.where(eye, g_row, 0.0), axis=1, keepdims=True)
    beta_col = jnp.sum(jnp.where(eye, beta_row, 0.0), axis=1, keepdims=True)
    gc_col = jnp.sum(jnp.where(causal, g_row, 0.0), axis=1, keepdims=True)
    gc_row = jnp.sum(jnp.where(causal_t, g_col, 0.0), axis=0, keepdims=True)
    g_last = jnp.sum(g_row, axis=1, keepdims=True)
    kb = k * beta_col
    return dict(q=q, k=k, kb=kb, masks=masks,
                decay=jnp.where(causal, jnp.exp(jnp.where(causal, gc_col - gc_row, 0.0)), 0.0),
                rhs=jnp.concatenate([v * beta_col, kb * jnp.exp(gc_col)], axis=1),
                q_dec=q * jnp.exp(gc_col),
                k_dec_t=(k * jnp.exp(g_last - gc_col)).T,
                s_decay=jnp.exp(g_last))


def _dn_solve_chunks(chunks):
    for ch in chunks:
        m = ch['masks']
        ch['a'] = jnp.where(m['strict'], _mm_nt(ch['kb'], ch['k']) * ch['decay'], 0.0)
        ch['t'] = jnp.where(m['eye'], 1.0, 0.0) - ch['a'] * m['levels'][0]
    for lvl in range(1, len(chunks[0]['masks']['levels'])):
        for ch in chunks:
            ch['mid'] = _mm(ch['a'] * ch['masks']['levels'][lvl], ch['t'])
        for ch in chunks:
            ch['t'] = ch['t'] - _mm(ch['t'], ch['mid'])
    for ch in chunks:
        sol = _mm(ch['t'], ch['rhs'])
        dv = sol.shape[1] // 2
        ch['u'], ch['w'] = sol[:, :dv], sol[:, dv:]
    for ch in chunks:
        ch['attn'] = _mm_nt(ch['q'], ch['k']) * ch['decay']


def _dn_advance(chunks, states):
    w_s = [_mm(ch['w'], s) for ch, s in zip(chunks, states)]
    q_s = [_mm(ch['q_dec'], s) for ch, s in zip(chunks, states)]
    v_new = [ch['u'] - ws for ch, ws in zip(chunks, w_s)]
    outs = [qs + _mm(ch['attn'], vn) for ch, qs, vn in zip(chunks, q_s, v_new)]
    new_states = [s * ch['s_decay'] + _mm(ch['k_dec_t'], vn) for ch, s, vn in zip(chunks, states, v_new)]
    return outs, new_states


def _dn_chunk_kernel(qf_ref, kf_ref, vf_ref, gf_ref, qb_ref, kb_ref, vb_ref, gb_ref, s0_ref,
                     of_ref, ob_ref, sfin_ref, s_ref):
    hg = pl.program_id(1)
    c = pl.program_id(2)
    hb = s_ref.shape[1]
    hdim = DN_HEAD_DIM

    @pl.when(c == 0)
    def _():
        s_ref[...] = s0_ref[0]

    n_sub = qf_ref.shape[0] // DN_CHUNK
    nh = 2 * DN_HEADS
    dirs = ((qf_ref, kf_ref, vf_ref, gf_ref, of_ref), (qb_ref, kb_ref, vb_ref, gb_ref, ob_ref))
    masks = [_dn_masks(DN_CHUNK, upper=False), _dn_masks(DN_CHUNK, upper=True)]
    chains = [(d, hh) for d in range(2) for hh in range(hb)]
    local = {}
    for d, hh in chains:
        q_ref, k_ref, v_ref, g_ref, _ = dirs[d]
        head = hg * hb + hh
        lanes = slice(hh * hdim, (hh + 1) * hdim)
        g_all = g_ref[pl.ds(d * DN_HEADS + head, 1), :]
        beta_all = g_ref[pl.ds(nh + d * DN_HEADS + head, 1), :]
        for j in range(n_sub):
            tok = slice(j * DN_CHUNK, (j + 1) * DN_CHUNK)
            local[d, hh, j] = _dn_chunk_local(q_ref[tok, lanes], k_ref[tok, lanes], v_ref[tok, lanes],
                                              g_all[:, tok], beta_all[:, tok], masks[d])
    _dn_solve_chunks(list(local.values()))
    states = [s_ref[d, hh] for d, hh in chains]
    for step in range(n_sub):
        subs = [step if d == 0 else n_sub - 1 - step for d, _ in chains]
        outs, states = _dn_advance([local[d, hh, j] for (d, hh), j in zip(chains, subs)], states)
        for (d, hh), j, o in zip(chains, subs, outs):
            dirs[d][4][j * DN_CHUNK:(j + 1) * DN_CHUNK, hh * hdim:(hh + 1) * hdim] = o
    for (d, hh), s in zip(chains, states):
        s_ref[d, hh] = s
    sfin_ref[0] = s_ref[...]


def dn_chunk(qkv, gates, s0, tok0, seq_len, blk=128, hb=8):
    b = s0.shape[0]
    nb = seq_len // blk
    hdim = DN_HEAD_DIM
    n_hg = DN_HEADS // hb
    fwd = lambda bi, c: (tok0 + bi * seq_len) // blk + c
    bwd = lambda bi, c: (tok0 + bi * seq_len) // blk + nb - 1 - c

    def col_spec(pos, o):
        return pl.BlockSpec((blk, hb * hdim), lambda bi, hg, c: (pos(bi, c), o * n_hg + hg))

    def specs(pos):
        return [col_spec(pos, o) for o in range(3)] + [
            pl.BlockSpec((4 * DN_HEADS, blk), lambda bi, hg, c: (0, pos(bi, c)))]

    st_spec = pl.BlockSpec((1, 2, hb, hdim, hdim), lambda bi, hg, c: (bi, 0, hg, 0, 0))
    o_shape = jax.ShapeDtypeStruct((b * seq_len, GROUP_W), F32)
    return pl.pallas_call(
        _dn_chunk_kernel,
        out_shape=(o_shape, o_shape, jax.ShapeDtypeStruct(s0.shape, F32)),
        grid=(b, n_hg, nb),
        in_specs=specs(fwd) + specs(bwd) + [st_spec],
        out_specs=(pl.BlockSpec((blk, hb * hdim), lambda bi, hg, c: (bi * nb + c, hg)),
                   pl.BlockSpec((blk, hb * hdim), lambda bi, hg, c: (bi * nb + nb - 1 - c, hg)),
                   st_spec),
        scratch_shapes=[pltpu.VMEM((2, hb, hdim, hdim), F32)],
        compiler_params=_cparams(("arbitrary", "arbitrary", "arbitrary")),
        name="dn_chunk",
    )(qkv, qkv, qkv, gates, qkv, qkv, qkv, gates, s0)


def _dn_post_kernel(ofc_ref, ofl_ref, obc_ref, obl_ref, gate_ref, g_ref, o_ref, *, nc):
    both = _region_pick(ofc_ref, ofl_ref, nc) + _region_pick(obc_ref, obl_ref, nc)
    for hh in range(DN_HEADS):
        sl = slice(hh * DN_HEAD_DIM, (hh + 1) * DN_HEAD_DIM)
        o = _head_rmsnorm(both[:, sl], g_ref[...])
        o_ref[:, sl] = (o * _silu(gate_ref[:, sl])).astype(o_ref.dtype)


def dn_post(o_f, o_b, z, norm_g, n_ctx_tok, tm=512):
    t = z.shape[0]
    blk = pl.BlockSpec((tm, GROUP_W), lambda i: (i, 0))
    pair = _region_specs(tm, GROUP_W, n_ctx_tok)
    return pl.pallas_call(
        functools.partial(_dn_post_kernel, nc=n_ctx_tok // tm),
        out_shape=jax.ShapeDtypeStruct((t, GROUP_W), BF16),
        grid=(t // tm,),
        in_specs=[*pair, *pair, pl.BlockSpec((tm, GROUP_W), lambda i: (i, Z_START['dn_gate'] // GROUP_W)),
                  pl.BlockSpec((1, DN_HEAD_DIM), lambda i: (0, 0))],
        out_specs=blk,
        compiler_params=_cparams(("arbitrary",)),
        name="dn_post",
    )(*o_f, *o_b, z, norm_g.reshape(1, -1))


def _rope_tables(length, dim):
    n_rows = length // GRID_W
    row = jnp.repeat(jnp.arange(n_rows), GRID_W).astype(F32)
    col = jnp.tile(jnp.arange(GRID_W), n_rows).astype(F32)
    quarter = dim // 4
    freqs = ROPE_THETA ** (-jnp.arange(quarter, dtype=F32) / quarter)
    ang = jnp.concatenate([row[:, None] * freqs, col[:, None] * freqs], axis=-1)
    return jnp.cos(ang), jnp.sin(ang)


def _permute_w_in(w):
    cols = [w[:, Z_REF_START[n]:Z_REF_START[n] + Z_WIDTH[n]] for n in _Z_ORDER]
    used = sum(Z_WIDTH[n] for n in _Z_ORDER)
    cols.append(jnp.zeros((w.shape[0], Z_COLS - used), w.dtype))
    return jnp.concatenate(cols, axis=1).astype(BF16)


def _pad_keys(keys):
    h, two, n, half = keys.shape
    z = jnp.zeros((h, n, half), keys.dtype)
    k0 = jnp.concatenate([keys[:, 0], z], axis=-1)
    k1 = jnp.concatenate([z, keys[:, 1]], axis=-1)
    return jnp.stack([k0, k1], axis=1).reshape(2 * h, n, 2 * half).astype(BF16)


def kernel(x_prompt, x_sample, c, cache_gqa_k, cache_gqa_v, cache_diff_k, cache_diff_v, state_s5_re, state_s5_im, state_delta, c_ctx, w_mod, b_mod, norm1_g, norm2_g, w_in, s5_lambda_re, s5_lambda_im, s5_log_dt, s5_b_re, s5_b_im, s5_c_re, s5_c_im, s5_d, s5_w_glu, dn_conv_w, dn_a_log, dn_dt_bias, dn_norm_g, gqa_q_norm, gqa_k_norm, diff_lambda, diff_subln_g, w_out, peer_w_q, peer_keys, peer_u, peer_v, final_norm_g):
    batch, seq, d = x_prompt.shape
    dec_batch, lat_len, _ = x_sample.shape
    depth = w_in.shape[0]
    n_ctx_tok = batch * seq
    n_lat_tok = dec_batch * lat_len
    tab_g = _rope_lane_tables(lat_len, GQA_HEAD_DIM, n_ctx_tok, dec_batch)
    tab_d = _rope_lane_tables(lat_len, DIFF_QK_DIM, n_ctx_tok, dec_batch)
    kv_w = GQA_KV_HEADS * GQA_HEAD_DIM

    def zcol(z, name, lo, hi):
        return z[lo:hi, Z_START[name]:Z_START[name] + Z_WIDTH[name]]

    def with_cache(cache, new, width):
        return jnp.concatenate([cache.reshape(dec_batch, -1, width), new.reshape(dec_batch, lat_len, width)],
                               axis=1).astype(BF16)

    cond = jnp.concatenate([c_ctx[None], c, jnp.zeros((8 - 1 - dec_batch, d), F32)], axis=0)
    mod_all = modulation(cond, w_mod, b_mod).reshape(depth, 8, N_MOD, d)

    x = jnp.concatenate([x_prompt.reshape(n_ctx_tok, d), x_sample.reshape(dec_batch * lat_len, d)], axis=0)
    new_state = [[] for _ in range(7)]
    for l in range(depth):
        mod = mod_all[l]
        z = in_projection(x, norm1_g[l], mod, _permute_w_in(w_in[l]), n_ctx_tok, lat_len)

        wb, wc, coef = _s5_tables(s5_lambda_re[l], s5_lambda_im[l], s5_log_dt[l],
                                  s5_b_re[l], s5_b_im[l], s5_c_re[l], s5_c_im[l])
        h0_ctx = jnp.zeros((batch, 2, S5_BLOCKS, 1, 2 * S5_BLK_STATE), F32)
        yf_c, yb_c, s5_fin = s5_scan(z, wb, wc, coef, h0_ctx, 0, seq, tc=seq)
        yf_l, yb_l, _ = s5_scan(z, wb, wc, coef, _s5_state_in(state_s5_re[:, l], state_s5_im[:, l]),
                                n_ctx_tok, lat_len, tc=256)
        y_a = s5_glu((yf_c, yf_l), (yb_c, yb_l), z, s5_d[l], s5_w_glu[l].astype(BF16), n_ctx_tok)
        s5_re, s5_im = _s5_state_out(s5_fin)

        qkv = dn_conv(z, dn_conv_w[l], n_ctx_tok, seq, lat_len)
        gates = dn_gates(z, dn_a_log[l], dn_dt_bias[l])
        s0_ctx = jnp.zeros((batch, 2, DN_HEADS, DN_HEAD_DIM, DN_HEAD_DIM), F32)
        of_c, ob_c, delta = dn_chunk(qkv, gates, s0_ctx, 0, seq)
        of_l, ob_l, _ = dn_chunk(qkv, gates, state_delta[:, l], n_ctx_tok, lat_len)
        y_b = dn_post((of_c, of_l), (ob_c, ob_l), z, dn_norm_g[l], n_ctx_tok)

        kg, kd = key_prep(z, gqa_k_norm[l], (*tab_g, *tab_d))
        gk_ctx, gv_ctx = kg[:n_ctx_tok], zcol(z, 'gv', 0, n_ctx_tok)
        fk_ctx, fv_ctx = kd[:n_ctx_tok], zcol(z, 'fv', 0, n_ctx_tok)
        yc_c = gqa_attention(z, gqa_q_norm[l], *tab_g, gk_ctx.reshape(batch, seq, kv_w).astype(BF16),
                             gv_ctx.reshape(batch, seq, kv_w).astype(BF16), 0, seq, tq=seq)
        yc_l = gqa_attention(z, gqa_q_norm[l], *tab_g, with_cache(cache_gqa_k[:, l], kg[n_ctx_tok:], kv_w),
                             with_cache(cache_gqa_v[:, l], zcol(z, 'gv', n_ctx_tok, None), kv_w),
                             n_ctx_tok, lat_len, tq=128)
        yd_c = diff_attention(z, *tab_d, fk_ctx.reshape(batch, seq, GROUP_W).astype(BF16),
                              fv_ctx.reshape(batch, seq, GROUP_W).astype(BF16),
                              diff_lambda[l], diff_subln_g[l], l, 0, seq, tq=seq)
        yd_l = diff_attention(z, *tab_d, with_cache(cache_diff_k[:, l], kd[n_ctx_tok:], GROUP_W),
                              with_cache(cache_diff_v[:, l], zcol(z, 'fv', n_ctx_tok, None), GROUP_W),
                              diff_lambda[l], diff_subln_g[l], l, n_ctx_tok, lat_len, tq=256)
        ctx_out = (gk_ctx.reshape(batch, seq, GQA_KV_HEADS, GQA_HEAD_DIM),
                   gv_ctx.reshape(batch, seq, GQA_KV_HEADS, GQA_HEAD_DIM),
                   fk_ctx.reshape(batch, seq, DIFF_HEADS, 2, DIFF_QK_DIM),
                   fv_ctx.reshape(batch, seq, DIFF_HEADS, DIFF_V_DIM),
                   s5_re, s5_im, delta)

        x = out_projection(y_a, y_b, (yc_c, yc_l), (yd_c, yd_l), w_out[l].astype(BF16), x, mod,
                           n_ctx_tok, lat_len)
        h2, s2, e2, thr, e1 = peer_query(x, norm2_g[l], mod, peer_w_q[l].astype(BF16), _pad_keys(peer_keys[l]),
                                         n_ctx_tok, lat_len)
        peer_t = peer_main(h2, peer_u[l].astype(BF16), peer_v[l].astype(BF16).T, s2, e2, thr, e1)
        residual = functools.partial(peer_residual, x, peer_t, mod, final_norm_g, n_ctx_tok, lat_len)
        if l < depth - 1:
            x = residual(False, 0, n_ctx_tok + n_lat_tok)
        else:
            y_prompt = residual(True, 0, n_ctx_tok).reshape(batch, seq, d)
            y_sample = residual(True, n_ctx_tok, n_lat_tok).reshape(dec_batch, lat_len, d)
        for acc, val in zip(new_state, ctx_out):
            acc.append(val)
    return (y_prompt, y_sample) + tuple(jnp.stack(s, axis=1) for s in new_state)
```

```python
import functools
import math

import jax
import jax.numpy as jnp
from jax import lax
from jax.experimental import pallas as pl
from jax.experimental.pallas import tpu as pltpu

F32 = jnp.float32
BF16 = jnp.bfloat16

D_MODEL = 4096
GRID_W = 64
GROUP_W = D_MODEL // 4
S5_WIDTH = GROUP_W
S5_CH_PER_GROUP = 16
S5_GROUPS = S5_WIDTH // S5_CH_PER_GROUP
S5_STATE = 64
DN_HEADS = 8
DN_HEAD_DIM = GROUP_W // DN_HEADS
DN_CHUNK = 64
GQA_HEADS = 8
GQA_KV_HEADS = 2
GQA_HEAD_DIM = GROUP_W // GQA_HEADS
DIFF_HEADS = 8
DIFF_V_DIM = GROUP_W // DIFF_HEADS
DIFF_QK_DIM = DIFF_V_DIM // 2
Q_BLOCK = 128
ROPE_THETA = 10000.0
PEER_HEADS = 8
PEER_KEY_DIM = 128
PEER_N_KEYS = 128
PEER_TOPK = 16
N_MOD = 6
EPS = 1e-6

_REF_SPLITS = (('s5_u', 1024), ('dn_q', 1024), ('dn_k', 1024), ('dn_v', 1024), ('dn_gate', 1024),
               ('dn_a', 16), ('dn_b', 16), ('gq', 1024), ('gk', 256), ('gv', 256),
               ('fq', 1024), ('fk', 1024), ('fv', 1024))
_Z_ORDER = ('s5_u', 'dn_q', 'dn_k', 'dn_v', 'dn_gate', 'gq', 'fq', 'fk', 'fv', 'gk', 'gv', 'dn_a', 'dn_b')
LANES = 128
IN_PROJ_TN = 1024
VMEM_LIMIT = 56 * 1024 * 1024


def _z_layout():
    widths = dict(_REF_SPLITS)
    ref_start, s = {}, 0
    for name, w in _REF_SPLITS:
        ref_start[name] = s
        s += w
    z_start, s = {}, 0
    for name in _Z_ORDER:
        z_start[name] = s
        s += widths[name]
    total = -(-s // IN_PROJ_TN) * IN_PROJ_TN
    return widths, ref_start, z_start, total


Z_WIDTH, Z_REF_START, Z_START, Z_COLS = _z_layout()


def _cparams(sem):
    return pltpu.CompilerParams(dimension_semantics=sem, vmem_limit_bytes=VMEM_LIMIT)


def _silu(x):
    return x * (1.0 / (1.0 + jnp.exp(-x)))


def _gelu_tanh(x):
    return 0.5 * x * (1.0 + jnp.tanh(math.sqrt(2.0 / math.pi) * (x + 0.044715 * (x * x * x))))


def _mod_kernel(cond_ref, w_ref, b_ref, o_ref):
    a = _silu(cond_ref[...]).astype(BF16)
    o_ref[0] = jnp.dot(a, w_ref[0].astype(BF16), preferred_element_type=F32) + b_ref[0]


def modulation(cond, w_mod, b_mod, tn=512):
    depth, d, n = w_mod.shape
    return pl.pallas_call(
        _mod_kernel,
        out_shape=jax.ShapeDtypeStruct((depth, 8, n), F32),
        grid=(depth, n // tn),
        in_specs=[pl.BlockSpec((8, d), lambda l, j: (0, 0)),
                  pl.BlockSpec((1, d, tn), lambda l, j: (l, 0, j)),
                  pl.BlockSpec((1, 1, tn), lambda l, j: (l, 0, j))],
        out_specs=pl.BlockSpec((1, 8, tn), lambda l, j: (l, 0, j)),
        compiler_params=_cparams(("arbitrary", "arbitrary")),
        name="modulation",
    )(cond, w_mod, b_mod.reshape(depth, 1, n))


def _mod_row(i, tile, n_ctx_tok, lat_len):
    t0 = i * tile
    return jnp.where(t0 < n_ctx_tok, 0, 1 + (t0 - n_ctx_tok) // lat_len)


def _norm_mod(x, g, shift, scale):
    r = lax.rsqrt(jnp.mean(x * x, axis=-1, keepdims=True) + EPS)
    return (x * r * g) * (1.0 + scale) + shift


def _inproj_kernel(x_ref, g_ref, mod_ref, w_ref, o_ref, h_ref):
    @pl.when(pl.program_id(1) == 0)
    def _():
        m = mod_ref[0]
        h_ref[...] = _norm_mod(x_ref[...], g_ref[...], m[0:1], m[1:2]).astype(BF16)

    o_ref[...] = jnp.dot(h_ref[...], w_ref[...], preferred_element_type=F32).astype(o_ref.dtype)


def in_projection(x, g, mod, w, n_ctx_tok, lat_len, tm=512, tn=IN_PROJ_TN):
    t, d = x.shape
    n = w.shape[1]
    row = functools.partial(_mod_row, tile=tm, n_ctx_tok=n_ctx_tok, lat_len=lat_len)
    return pl.pallas_call(
        _inproj_kernel,
        out_shape=jax.ShapeDtypeStruct((t, n), F32),
        grid=(t // tm, n // tn),
        in_specs=[pl.BlockSpec((tm, d), lambda i, j: (i, 0)),
                  pl.BlockSpec((1, d), lambda i, j: (0, 0)),
                  pl.BlockSpec((1, N_MOD, d), lambda i, j: (row(i), 0, 0)),
                  pl.BlockSpec((d, tn), lambda i, j: (0, j))],
        out_specs=pl.BlockSpec((tm, tn), lambda i, j: (i, j)),
        scratch_shapes=[pltpu.VMEM((tm, d), BF16)],
        compiler_params=_cparams(("arbitrary", "arbitrary")),
        name="in_projection",
    )(x, g.reshape(1, d), mod, w)


def _outproj_kernel(ya_ref, yb_ref, ycc_ref, ycl_ref, ydc_ref, ydl_ref, w_ref, x_ref, mod_ref, o_ref, *, nc):
    ys = (ya_ref[...], yb_ref[...], _region_pick(ycc_ref, ycl_ref, nc), _region_pick(ydc_ref, ydl_ref, nc))
    acc = None
    for m, y in enumerate(ys):
        part = jnp.dot(y, w_ref[m * GROUP_W:(m + 1) * GROUP_W, :], preferred_element_type=F32)
        acc = part if acc is None else acc + part
    o_ref[...] = x_ref[...] + mod_ref[0][2:3] * acc


def out_projection(y_a, y_b, y_c, y_d, w, x, mod, n_ctx_tok, lat_len, tm=512, tn=1024):
    t, n = x.shape
    kdim = w.shape[0]
    row = functools.partial(_mod_row, tile=tm, n_ctx_tok=n_ctx_tok, lat_len=lat_len)
    y_spec = pl.BlockSpec((tm, GROUP_W), lambda i, j: (i, 0))
    pair = _region_specs(tm, GROUP_W, n_ctx_tok)
    return pl.pallas_call(
        functools.partial(_outproj_kernel, nc=n_ctx_tok // tm),
        out_shape=jax.ShapeDtypeStruct((t, n), F32),
        grid=(t // tm, n // tn),
        in_specs=[y_spec, y_spec, *pair, *pair,
                  pl.BlockSpec((kdim, tn), lambda i, j: (0, j)),
                  pl.BlockSpec((tm, tn), lambda i, j: (i, j)),
                  pl.BlockSpec((1, N_MOD, tn), lambda i, j: (row(i), 0, j))],
        out_specs=pl.BlockSpec((tm, tn), lambda i, j: (i, j)),
        compiler_params=_cparams(("arbitrary", "arbitrary")),
        name="out_projection",
    )(y_a, y_b, *y_c, *y_d, w, x, mod)


def _top_rows(x, k):
    vals = []
    cur = x
    for _ in range(k):
        mx = jnp.max(cur, axis=0, keepdims=True)
        vals.append(mx)
        cur = jnp.where(cur == mx, -jnp.inf, cur)
    return jnp.concatenate(vals, axis=0)


def _pair_sums(a, b):
    k, cols = a.shape
    sub = 8
    pad_rows = -(-k // sub) * sub - k
    neg = jnp.full((pad_rows, cols), -jnp.inf, F32)
    a_pad = jnp.concatenate([a, neg], axis=0)
    b_pad = jnp.concatenate([b, neg], axis=0)
    pieces = [a[0:1] + b_pad]
    row = lax.broadcasted_iota(jnp.int32, (sub, cols), 0)
    for i in range(1, sub):
        pieces.append(jnp.where(row < k // (i + 1), a[i:i + 1] + b_pad[0:sub], -jnp.inf))
    pieces.append(a_pad[sub:] + b[0:1])
    return jnp.concatenate(pieces, axis=0)


def _peer_query_kernel(x_ref, g_ref, mod_ref, wq_ref, keys_ref,
                       h2t_ref, s2_ref, e2_ref, thr_ref, e1_ref):
    m = mod_ref[0]
    h = _norm_mod(x_ref[...], g_ref[...], m[3:4], m[4:5])
    hb = h.astype(BF16)
    h2t_ref[...] = h.T.astype(BF16)
    q = jnp.dot(hb, wq_ref[...], preferred_element_type=F32)
    nt = (((1,), (1,)), ((), ()))
    for hd in range(PEER_HEADS):
        qh = q[:, hd * PEER_KEY_DIM:(hd + 1) * PEER_KEY_DIM].astype(BF16)
        s1 = lax.dot_general(keys_ref[2 * hd], qh, nt, preferred_element_type=F32)
        s2 = lax.dot_general(keys_ref[2 * hd + 1], qh, nt, preferred_element_type=F32)
        a = _top_rows(s1, PEER_TOPK + 1)
        b = _top_rows(s2, PEER_TOPK + 1)
        v = _top_rows(_pair_sums(a, b), PEER_TOPK + 1)
        z = jnp.sum(jnp.exp(v[:PEER_TOPK] - v[0:1]), axis=0, keepdims=True)
        tau = 0.5 * (v[PEER_TOPK - 1:PEER_TOPK] + v[PEER_TOPK:PEER_TOPK + 1])
        s2_ref[hd] = s2
        e2_ref[hd] = jnp.exp(s2 - b[0:1])
        thr_ref[hd] = tau - s1
        e1_ref[hd] = jnp.exp(s1 - a[0:1]) / z


def peer_query(x, g, mod, wq, keys_pad, n_ctx_tok, lat_len, tq=256):
    t, d = x.shape
    row = functools.partial(_mod_row, tile=tq, n_ctx_tok=n_ctx_tok, lat_len=lat_len)
    aux = jax.ShapeDtypeStruct((PEER_HEADS, PEER_N_KEYS, t), F32)
    aux_spec = pl.BlockSpec((PEER_HEADS, PEER_N_KEYS, tq), lambda i: (0, 0, i))
    return pl.pallas_call(
        _peer_query_kernel,
        out_shape=(jax.ShapeDtypeStruct((d, t), BF16), aux, aux, aux, aux),
        grid=(t // tq,),
        in_specs=[pl.BlockSpec((tq, d), lambda i: (i, 0)),
                  pl.BlockSpec((1, d), lambda i: (0, 0)),
                  pl.BlockSpec((1, N_MOD, d), lambda i: (row(i), 0, 0)),
                  pl.BlockSpec(wq.shape, lambda i: (0, 0)),
                  pl.BlockSpec(keys_pad.shape, lambda i: (0, 0, 0))],
        out_specs=(pl.BlockSpec((d, tq), lambda i: (0, i)), aux_spec, aux_spec, aux_spec, aux_spec),
        compiler_params=_cparams(("arbitrary",)),
        name="peer_query",
    )(x, g.reshape(1, d), mod, wq, keys_pad)


def _cast_rows_kernel(x_ref, o_ref):
    o_ref[...] = x_ref[0].astype(o_ref.dtype)


def _cast_transpose_kernel(x_ref, o_ref):
    o_ref[...] = x_ref[0].T.astype(o_ref.dtype)


def table_bf16(tabs, layer, transpose, tr=512):
    _, n, d = tabs.shape
    return pl.pallas_call(
        _cast_transpose_kernel if transpose else _cast_rows_kernel,
        out_shape=jax.ShapeDtypeStruct((d, n) if transpose else (n, d), BF16),
        grid=(n // tr,),
        in_specs=[pl.BlockSpec((1, tr, d), lambda i: (layer, i, 0))],
        out_specs=pl.BlockSpec((d, tr), lambda i: (0, i)) if transpose else pl.BlockSpec((tr, d), lambda i: (i, 0)),
        compiler_params=_cparams(("arbitrary",)),
        name="table_bf16",
    )(tabs)


def _peer_main_kernel(xt_ref, u_ref, vt_ref, s2_ref, e2_ref, thr_ref, e1_ref, o_ref, *, n_sub):
    k = pl.program_id(1)

    @pl.when(k == 0)
    def _():
        o_ref[...] = jnp.zeros_like(o_ref)

    ws = []
    for r in range(n_sub):
        i1 = k * n_sub + r
        w = None
        for hd in range(PEER_HEADS):
            thr = thr_ref[hd, pl.ds(i1, 1), :]
            e1 = e1_ref[hd, pl.ds(i1, 1), :]
            term = jnp.where(s2_ref[hd] >= thr, e2_ref[hd], 0.0) * e1
            w = term if w is None else w + term
        ws.append(w)
    w = jnp.concatenate(ws, axis=0) if n_sub > 1 else ws[0]

    hid = jnp.dot(u_ref[...], xt_ref[...], preferred_element_type=F32)
    hw = (_gelu_tanh(hid) * w).astype(BF16)
    o_ref[...] += jnp.dot(vt_ref[...], hw, preferred_element_type=F32)


def peer_main(h2t, u, vt, s2, e2, thr, e1, tt=512, te=512):
    d, t = h2t.shape
    n_exp = u.shape[0]
    n_sub = te // PEER_N_KEYS
    once = pl.Buffered(1)
    aux_spec = pl.BlockSpec((PEER_HEADS, PEER_N_KEYS, tt), lambda i, k: (0, 0, i), pipeline_mode=once)
    return pl.pallas_call(
        functools.partial(_peer_main_kernel, n_sub=n_sub),
        out_shape=jax.ShapeDtypeStruct((d, t), F32),
        grid=(t // tt, n_exp // te),
        in_specs=[pl.BlockSpec((d, tt), lambda i, k: (0, i), pipeline_mode=once),
                  pl.BlockSpec((te, d), lambda i, k: (k, 0)),
                  pl.BlockSpec((d, te), lambda i, k: (0, k)),
                  aux_spec, aux_spec, aux_spec, aux_spec],
        out_specs=pl.BlockSpec((d, tt), lambda i, k: (0, i), pipeline_mode=once),
        compiler_params=_cparams(("arbitrary", "arbitrary")),
        name="peer_main",
    )(h2t, u, vt, s2, e2, thr, e1)


def _peer_residual_kernel(x_ref, pt_ref, mod_ref, g_ref, o_ref, *, final):
    x = x_ref[...] + mod_ref[0][5:6] * pt_ref[...].T
    if final:
        r = lax.rsqrt(jnp.mean(x * x, axis=-1, keepdims=True) + EPS)
        x = x * r * g_ref[...]
    o_ref[...] = x


def peer_residual(x, peer_t, mod, g_final, n_ctx_tok, lat_len, final, tok0, n_tok, tr=256):
    d = x.shape[1]
    first = tok0 // tr
    mod_row = functools.partial(_mod_row, tile=tr, n_ctx_tok=n_ctx_tok, lat_len=lat_len)
    row = lambda i: mod_row(first + i)
    return pl.pallas_call(
        functools.partial(_peer_residual_kernel, final=final),
        out_shape=jax.ShapeDtypeStruct((n_tok, d), F32),
        grid=(n_tok // tr,),
        in_specs=[pl.BlockSpec((tr, d), lambda i: (first + i, 0)),
                  pl.BlockSpec((d, tr), lambda i: (0, first + i)),
                  pl.BlockSpec((1, N_MOD, d), lambda i: (row(i), 0, 0)),
                  pl.BlockSpec((1, d), lambda i: (0, 0))],
        out_specs=pl.BlockSpec((tr, d), lambda i: (i, 0)),
        compiler_params=_cparams(("arbitrary",)),
        name="peer_residual",
    )(x, peer_t, mod, g_final.reshape(1, d))


_NT = (((1,), (1,)), ((), ()))


def _head_rmsnorm(x, g):
    return x * lax.rsqrt(jnp.mean(x * x, axis=-1, keepdims=True) + EPS) * g


def _rope_half(x, cosf, sinf):
    return x * cosf + pltpu.roll(x, GQA_HEAD_DIM // 2, 1) * sinf


def _rope_quarter(x, cosf, sinf):
    lane = lax.broadcasted_iota(jnp.int32, x.shape, 1)
    half = DIFF_QK_DIM // 2
    swapped = jnp.where((lane % DIFF_QK_DIM) < half,
                        pltpu.roll(x, LANES - half, 1), pltpu.roll(x, half, 1))
    return x * cosf + swapped * sinf


def _key_prep_kernel(gk_ref, fk_ref, kn_ref, cg_ref, sg_ref, cd_ref, sd_ref, kg_ref, kd_ref):
    for hh in range(GQA_KV_HEADS):
        sl = slice(hh * GQA_HEAD_DIM, (hh + 1) * GQA_HEAD_DIM)
        kg_ref[:, sl] = _rope_half(_head_rmsnorm(gk_ref[:, sl], kn_ref[...]), cg_ref[...], sg_ref[...])
    for hh in range(DIFF_HEADS):
        sl = slice(hh * LANES, (hh + 1) * LANES)
        kd_ref[:, sl] = _rope_quarter(fk_ref[:, sl], cd_ref[...], sd_ref[...])


def key_prep(z, k_norm, tabs, tm=256):
    t = z.shape[0]
    gk_blk = Z_START['gk'] // Z_WIDTH['gk']
    fk_blk = Z_START['fk'] // Z_WIDTH['fk']
    tab_spec = pl.BlockSpec((tm, LANES), lambda i: (i, 0))
    return pl.pallas_call(
        _key_prep_kernel,
        out_shape=(jax.ShapeDtypeStruct((t, Z_WIDTH['gk']), F32), jax.ShapeDtypeStruct((t, Z_WIDTH['fk']), F32)),
        grid=(t // tm,),
        in_specs=[pl.BlockSpec((tm, Z_WIDTH['gk']), lambda i: (i, gk_blk)),
                  pl.BlockSpec((tm, Z_WIDTH['fk']), lambda i: (i, fk_blk)),
                  pl.BlockSpec((1, GQA_HEAD_DIM), lambda i: (0, 0)),
                  tab_spec, tab_spec, tab_spec, tab_spec],
        out_specs=(pl.BlockSpec((tm, Z_WIDTH['gk']), lambda i: (i, 0)),
                   pl.BlockSpec((tm, Z_WIDTH['fk']), lambda i: (i, 0))),
        compiler_params=_cparams(("arbitrary",)),
        name="key_prep",
    )(z, z, k_norm.reshape(1, -1), *tabs)


def _softmax_rows(s):
    m = jnp.max(s, axis=-1, keepdims=True)
    p = jnp.exp(s - m)
    return p, jnp.sum(p, axis=-1, keepdims=True)


def _gqa_kernel(q_ref, qn_ref, cos_ref, sin_ref, k_ref, v_ref, o_ref):
    tq = q_ref.shape[0]
    n_rep = GQA_HEADS // GQA_KV_HEADS
    qs = []
    for hh in range(n_rep):
        x = _head_rmsnorm(q_ref[:, hh * GQA_HEAD_DIM:(hh + 1) * GQA_HEAD_DIM], qn_ref[...])
        x = _rope_half(x, cos_ref[...], sin_ref[...])
        qs.append((x * (GQA_HEAD_DIM ** -0.5)).astype(BF16))
    q = jnp.concatenate(qs, axis=0)
    s = lax.dot_general(q, k_ref[0], _NT, preferred_element_type=F32)
    p, l = _softmax_rows(s)
    o = jnp.dot(p.astype(BF16), v_ref[0], preferred_element_type=F32) / l
    o_ref[...] = jnp.concatenate([o[hh * tq:(hh + 1) * tq] for hh in range(n_rep)], axis=1).astype(o_ref.dtype)


def gqa_attention(z, q_norm, cos, sin, keys, vals, tok0, seq_len, tq):
    b, lk, _ = keys.shape
    n_rep = GQA_HEADS // GQA_KV_HEADS
    qw = n_rep * GQA_HEAD_DIM
    q_blk = Z_START['gq'] // qw
    nq = seq_len // tq
    row = lambda bi, g, qi: (tok0 + bi * seq_len) // tq + qi
    kv_spec = pl.BlockSpec((1, lk, GQA_HEAD_DIM), lambda bi, g, qi: (bi, 0, g))
    tab_spec = pl.BlockSpec((tq, LANES), lambda bi, g, qi: (row(bi, g, qi), 0))
    return pl.pallas_call(
        _gqa_kernel,
        out_shape=jax.ShapeDtypeStruct((b * seq_len, GROUP_W), BF16),
        grid=(b, GQA_KV_HEADS, nq),
        in_specs=[pl.BlockSpec((tq, qw), lambda bi, g, qi: (row(bi, g, qi), q_blk + g)),
                  pl.BlockSpec((1, GQA_HEAD_DIM), lambda bi, g, qi: (0, 0)),
                  tab_spec, tab_spec, kv_spec, kv_spec],
        out_specs=pl.BlockSpec((tq, qw), lambda bi, g, qi: (bi * nq + qi, g)),
        compiler_params=_cparams(("arbitrary", "arbitrary", "arbitrary")),
        name="gqa_attention",
    )(z, q_norm.reshape(1, -1), cos, sin, keys, vals)


def _diff_kernel(q_ref, cos_ref, sin_ref, k_ref, v_ref, lp_ref, g_ref, o_ref, *, lam_init):
    tq = q_ref.shape[0]
    lp = lp_ref[...]
    lam = (jnp.exp(jnp.sum(lp[0:1] * lp[1:2], axis=1, keepdims=True))
           - jnp.exp(jnp.sum(lp[2:3] * lp[3:4], axis=1, keepdims=True)) + lam_init)
    q = _rope_quarter(q_ref[...], cos_ref[...], sin_ref[...]) * (DIFF_QK_DIM ** -0.5)
    lane = lax.broadcasted_iota(jnp.int32, q.shape, 1)
    q12 = jnp.concatenate([jnp.where(lane < DIFF_QK_DIM, q, 0.0), jnp.where(lane >= DIFF_QK_DIM, q, 0.0)], axis=0)
    s = lax.dot_general(q12.astype(BF16), k_ref[0], _NT, preferred_element_type=F32)
    p, l = _softmax_rows(s)
    p = p / l
    w = p[:tq] - lam * p[tq:]
    o = jnp.dot(w.astype(BF16), v_ref[0], preferred_element_type=F32)
    o_ref[...] = (_head_rmsnorm(o, g_ref[...]) * (1.0 - lam_init)).astype(o_ref.dtype)


def diff_attention(z, cos, sin, keys, vals, lam_params, subln_g, layer_idx, tok0, seq_len, tq):
    b, lk, _ = keys.shape
    q_blk = Z_START['fq'] // LANES
    nq = seq_len // tq
    lam_init = 0.8 - 0.6 * math.exp(-0.3 * layer_idx)
    row = lambda bi, h, qi: (tok0 + bi * seq_len) // tq + qi
    kv_spec = pl.BlockSpec((1, lk, LANES), lambda bi, h, qi: (bi, 0, h))
    tab_spec = pl.BlockSpec((tq, LANES), lambda bi, h, qi: (row(bi, h, qi), 0))
    return pl.pallas_call(
        functools.partial(_diff_kernel, lam_init=lam_init),
        out_shape=jax.ShapeDtypeStruct((b * seq_len, GROUP_W), BF16),
        grid=(b, DIFF_HEADS, nq),
        in_specs=[pl.BlockSpec((tq, LANES), lambda bi, h, qi: (row(bi, h, qi), q_blk + h)),
                  tab_spec, tab_spec, kv_spec, kv_spec,
                  pl.BlockSpec(lam_params.shape, lambda bi, h, qi: (0, 0)),
                  pl.BlockSpec((1, DIFF_V_DIM), lambda bi, h, qi: (0, 0))],
        out_specs=pl.BlockSpec((tq, LANES), lambda bi, h, qi: (bi * nq + qi, h)),
        compiler_params=_cparams(("arbitrary", "arbitrary", "arbitrary")),
        name="diff_attention",
    )(z, cos, sin, keys, vals, lam_params, subln_g.reshape(1, -1))


def _rope_lane_tables(length, dim, n_ctx_tok, dec_batch):
    cos, sin = _rope_tables(length, dim)
    reps = LANES // dim
    cosf = jnp.tile(jnp.concatenate([cos, cos], axis=1), (dec_batch, reps))
    sinf = jnp.tile(jnp.concatenate([-sin, sin], axis=1), (dec_batch, reps))
    ones = jnp.ones((n_ctx_tok, LANES), F32)
    return jnp.concatenate([ones, cosf], axis=0), jnp.concatenate([jnp.zeros_like(ones), sinf], axis=0)


S5_BLK_GROUPS = LANES // S5_CH_PER_GROUP
S5_BLOCKS = S5_GROUPS // S5_BLK_GROUPS
S5_BLK_STATE = S5_BLK_GROUPS * S5_STATE
SUBLANES = 8


def _s5_tables(lam_re, lam_im, log_dt, b_re, b_im, c_re, c_im):
    dt = jnp.exp(log_dt)[..., None]
    mag = jnp.exp(lam_re * dt)
    ab_re, ab_im = mag * jnp.cos(lam_im * dt), mag * jnp.sin(lam_im * dt)
    den = lam_re * lam_re + lam_im * lam_im
    nr = ab_re - 1.0
    coef_re = (nr * lam_re + ab_im * lam_im) / den
    coef_im = (ab_im * lam_re - nr * lam_im) / den
    bb_re = coef_re[..., None] * b_re - coef_im[..., None] * b_im
    bb_im = coef_re[..., None] * b_im + coef_im[..., None] * b_re
    j, gl, p, h = S5_BLOCKS, S5_BLK_GROUPS, S5_STATE, S5_CH_PER_GROUP
    eye = jnp.eye(gl, dtype=F32)

    def in_blocks(bb):
        t = bb.reshape(2, j, gl, p, h)
        return jnp.einsum('djgph,gk->djghkp', t, eye).reshape(2, j, gl * h, gl * p)

    def out_blocks(cc):
        t = cc.reshape(2, j, gl, h, p)
        return jnp.einsum('djghp,gk->djgpkh', t, eye).reshape(2, j, gl * p, gl * h)

    wb = jnp.concatenate([in_blocks(bb_re), in_blocks(bb_im)], axis=-1).astype(BF16)
    wc = jnp.concatenate([out_blocks(c_re), out_blocks(-c_im)], axis=-2).astype(BF16)

    ar, ai = ab_re.reshape(2, j, 1, gl * p), ab_im.reshape(2, j, 1, gl * p)
    pows = [(ar, ai)]
    for _ in range(SUBLANES - 1):
        pr, pi = pows[-1]
        pows.append((pr * ar - pi * ai, pr * ai + pi * ar))
    r = jnp.arange(SUBLANES).reshape(1, 1, SUBLANES, 1)
    rows = []
    for d in range(2):
        dr = []
        for s in (1, 2, 4):
            mask = (r >= s) if d == 0 else (r <= SUBLANES - 1 - s)
            dr += [jnp.where(mask, pows[s - 1][0][d:d + 1], 0.0), jnp.where(mask, pows[s - 1][1][d:d + 1], 0.0)]
        order = range(SUBLANES) if d == 0 else range(SUBLANES - 1, -1, -1)
        dr.append(jnp.concatenate([pows[k][0][d:d + 1] for k in order], axis=2))
        dr.append(jnp.concatenate([pows[k][1][d:d + 1] for k in order], axis=2))
        rows.append(jnp.stack([jnp.broadcast_to(x, (1, j, SUBLANES, gl * p)) for x in dr], axis=2))
    return wb, wc, jnp.concatenate(rows, axis=0)


def _s5_scan_tile(xr, xi, coef_ref, d, hr, hi):
    for k, s in enumerate((1, 2, 4)):
        shift = s if d == 0 else SUBLANES - s
        ar, ai = coef_ref[d, 0, 2 * k], coef_ref[d, 0, 2 * k + 1]
        sr, si = pltpu.roll(xr, shift, 0), pltpu.roll(xi, shift, 0)
        xr, xi = xr + ar * sr - ai * si, xi + ar * si + ai * sr
    cr, ci = coef_ref[d, 0, 6], coef_ref[d, 0, 7]
    return xr + cr * hr - ci * hi, xi + cr * hi + ci * hr


def _s5_scan_kernel(uf_ref, ub_ref, wb_ref, wc_ref, coef_ref, h0_ref, yf_ref, yb_ref, hfin_ref,
                    sr_ref, si_ref, carry_ref):
    c = pl.program_id(2)
    bs = S5_BLK_STATE
    tc = uf_ref.shape[0]
    n_tiles = tc // SUBLANES

    @pl.when(c == 0)
    def _():
        carry_ref[...] = h0_ref[0, :, 0]

    for d, (u_ref, y_ref) in enumerate(((uf_ref, yf_ref), (ub_ref, yb_ref))):
        bu = jnp.dot(u_ref[...].astype(BF16), wb_ref[d, 0], preferred_element_type=F32)
        sr_ref[...] = bu[:, :bs]
        si_ref[...] = bu[:, bs:]

        def body(i, carry, d=d):
            hr, hi = carry
            tile = i if d == 0 else n_tiles - 1 - i
            rows = pl.ds(pl.multiple_of(tile * SUBLANES, SUBLANES), SUBLANES)
            xr, xi = _s5_scan_tile(sr_ref[rows, :], si_ref[rows, :], coef_ref, d, hr, hi)
            sr_ref[rows, :] = xr
            si_ref[rows, :] = xi
            last = SUBLANES - 1 if d == 0 else 0
            return xr[last:last + 1], xi[last:last + 1]

        h0 = carry_ref[d]
        hr, hi = lax.fori_loop(0, n_tiles, body, (h0[:, :bs], h0[:, bs:]))
        carry_ref[d] = jnp.concatenate([hr, hi], axis=1)
        hcat = jnp.concatenate([sr_ref[...], si_ref[...]], axis=1).astype(BF16)
        y_ref[...] = jnp.dot(hcat, wc_ref[d, 0], preferred_element_type=F32)

    hfin_ref[0, :, 0] = carry_ref[...]


def s5_scan(z, wb, wc, coef, h0, tok0, seq_len, tc):
    b = h0.shape[0]
    nt = seq_len // tc
    width = 2 * S5_BLK_STATE
    u_blk = Z_START['s5_u'] // LANES
    row_f = lambda bi, j, c: ((tok0 + bi * seq_len) // tc + c, u_blk + j)
    row_b = lambda bi, j, c: ((tok0 + bi * seq_len) // tc + nt - 1 - c, u_blk + j)
    st_spec = pl.BlockSpec((1, 2, 1, 1, width), lambda bi, j, c: (bi, 0, j, 0, 0))
    y_shape = jax.ShapeDtypeStruct((b * seq_len, S5_WIDTH), F32)
    return pl.pallas_call(
        _s5_scan_kernel,
        out_shape=(y_shape, y_shape, jax.ShapeDtypeStruct(h0.shape, F32)),
        grid=(b, S5_BLOCKS, nt),
        in_specs=[pl.BlockSpec((tc, LANES), row_f),
                  pl.BlockSpec((tc, LANES), row_b),
                  pl.BlockSpec((2, 1, LANES, width), lambda bi, j, c: (0, j, 0, 0)),
                  pl.BlockSpec((2, 1, width, LANES), lambda bi, j, c: (0, j, 0, 0)),
                  pl.BlockSpec((2, 1, SUBLANES, SUBLANES, S5_BLK_STATE), lambda bi, j, c: (0, j, 0, 0, 0)),
                  st_spec],
        out_specs=(pl.BlockSpec((tc, LANES), lambda bi, j, c: (bi * nt + c, j)),
                   pl.BlockSpec((tc, LANES), lambda bi, j, c: (bi * nt + nt - 1 - c, j)),
                   st_spec),
        scratch_shapes=[pltpu.VMEM((tc, S5_BLK_STATE), F32), pltpu.VMEM((tc, S5_BLK_STATE), F32),
                        pltpu.VMEM((2, 1, width), F32)],
        compiler_params=_cparams(("arbitrary", "arbitrary", "arbitrary")),
        name="s5_scan",
    )(z, z, wb, wc, coef, h0)


def _region_specs(tm, width, n_ctx_tok):
    nc = n_ctx_tok // tm
    return (pl.BlockSpec((tm, width), lambda i, *_: (jnp.minimum(i, nc - 1), 0)),
            pl.BlockSpec((tm, width), lambda i, *_: (jnp.maximum(i - nc, 0), 0)))


def _region_pick(ctx_ref, lat_ref, n_ctx_tiles):
    return jnp.where(pl.program_id(0) < n_ctx_tiles, ctx_ref[...], lat_ref[...])


def _s5_glu_kernel(yfc_ref, yfl_ref, ybc_ref, ybl_ref, u_ref, d_ref, w_ref, o_ref, *, nc):
    y = _gelu_tanh(_region_pick(yfc_ref, yfl_ref, nc) + _region_pick(ybc_ref, ybl_ref, nc)
                   + d_ref[...] * u_ref[...])
    gate = jnp.dot(y.astype(BF16), w_ref[...], preferred_element_type=F32)
    o_ref[...] = (y * (1.0 / (1.0 + jnp.exp(-gate)))).astype(o_ref.dtype)


def s5_glu(yf, yb, z, d_skip, w_glu, n_ctx_tok, tm=512):
    t = z.shape[0]
    blk = pl.BlockSpec((tm, S5_WIDTH), lambda i: (i, 0))
    pair = _region_specs(tm, S5_WIDTH, n_ctx_tok)
    return pl.pallas_call(
        functools.partial(_s5_glu_kernel, nc=n_ctx_tok // tm),
        out_shape=jax.ShapeDtypeStruct((t, S5_WIDTH), BF16),
        grid=(t // tm,),
        in_specs=[*pair, *pair, pl.BlockSpec((tm, S5_WIDTH), lambda i: (i, Z_START['s5_u'] // S5_WIDTH)),
                  pl.BlockSpec((1, S5_WIDTH), lambda i: (0, 0)),
                  pl.BlockSpec((S5_WIDTH, S5_WIDTH), lambda i: (0, 0))],
        out_specs=blk,
        compiler_params=_cparams(("arbitrary",)),
        name="s5_glu",
    )(*yf, *yb, z, d_skip.reshape(1, -1), w_glu)


def _s5_state_in(re, im):
    b = re.shape[0]
    r = re.reshape(b, 2, S5_BLOCKS, 1, S5_BLK_STATE)
    i = im.reshape(b, 2, S5_BLOCKS, 1, S5_BLK_STATE)
    return jnp.concatenate([r, i], axis=-1)


def _s5_state_out(h):
    b = h.shape[0]
    re = h[..., :S5_BLK_STATE].reshape(b, 2, S5_GROUPS, S5_STATE)
    im = h[..., S5_BLK_STATE:].reshape(b, 2, S5_GROUPS, S5_STATE)
    return re, im


def _dn_conv_kernel(x_ref, prev_ref, next_ref, w_ref, o_ref, *, n_ctx_tok, seq, lat_len):
    i = pl.program_id(0)
    cg = pl.program_id(1)
    tm = x_ref.shape[0]
    t0 = i * tm
    in_ctx = t0 < n_ctx_tok
    pos = jnp.where(in_ctx, t0 % seq, (t0 - n_ctx_tok) % lat_len)
    length = jnp.where(in_ctx, seq, lat_len)
    x = x_ref[...]
    row = lax.broadcasted_iota(jnp.int32, x.shape, 0)
    before = jnp.where(pos > 0, prev_ref[SUBLANES - 1:SUBLANES, :], 0.0)
    after = jnp.where(pos + tm < length, next_ref[0:1, :], 0.0)
    xp = jnp.where(row == 0, before, pltpu.roll(x, 1, 0))
    xn = jnp.where(row == tm - 1, after, pltpu.roll(x, tm - 1, 0))
    y = _silu(w_ref[0:1, :] * xp + w_ref[1:2, :] * x + w_ref[2:3, :] * xn)
    scale = jnp.where(cg == 0, DN_HEAD_DIM ** -0.5, 1.0)
    for hh in range(DN_HEADS):
        sl = slice(hh * DN_HEAD_DIM, (hh + 1) * DN_HEAD_DIM)
        yh = y[:, sl]
        nrm = yh * (lax.rsqrt(jnp.sum(yh * yh, axis=-1, keepdims=True) + EPS) * scale)
        o_ref[:, sl] = jnp.where(cg == 2, yh, nrm)


def dn_conv(z, conv_w, n_ctx_tok, seq, lat_len, tm=256):
    t = z.shape[0]
    first = Z_START['dn_q'] // GROUP_W
    nsub = tm // SUBLANES
    last_sub = t // SUBLANES - 1
    return pl.pallas_call(
        functools.partial(_dn_conv_kernel, n_ctx_tok=n_ctx_tok, seq=seq, lat_len=lat_len),
        out_shape=jax.ShapeDtypeStruct((t, 3 * GROUP_W), F32),
        grid=(t // tm, 3),
        in_specs=[pl.BlockSpec((tm, GROUP_W), lambda i, cg: (i, first + cg)),
                  pl.BlockSpec((SUBLANES, GROUP_W), lambda i, cg: (jnp.maximum(i * nsub - 1, 0), first + cg)),
                  pl.BlockSpec((SUBLANES, GROUP_W), lambda i, cg: (jnp.minimum((i + 1) * nsub, last_sub), first + cg)),
                  pl.BlockSpec((3, GROUP_W), lambda i, cg: (0, cg))],
        out_specs=pl.BlockSpec((tm, GROUP_W), lambda i, cg: (i, cg)),
        compiler_params=_cparams(("arbitrary", "arbitrary")),
        name="dn_conv",
    )(z, z, z, conv_w)


def _dn_gate_kernel(x_ref, neg_a_ref, dtb_ref, o_ref):
    xt = x_ref[...].T
    nh = 2 * DN_HEADS
    a = xt[0:nh] + dtb_ref[...]
    softplus = jnp.maximum(a, 0.0) + jnp.log(1.0 + jnp.exp(-jnp.abs(a)))
    o_ref[0:nh, :] = neg_a_ref[...] * softplus
    o_ref[nh:2 * nh, :] = 1.0 / (1.0 + jnp.exp(-xt[nh:2 * nh]))


def dn_gates(z, a_log, dt_bias, tm=256):
    t = z.shape[0]
    nh = 2 * DN_HEADS
    neg_a = jnp.broadcast_to(-jnp.exp(a_log).reshape(nh, 1), (nh, tm))
    dtb = jnp.broadcast_to(dt_bias.reshape(nh, 1), (nh, tm))
    return pl.pallas_call(
        _dn_gate_kernel,
        out_shape=jax.ShapeDtypeStruct((2 * nh, t), F32),
        grid=(t // tm,),
        in_specs=[pl.BlockSpec((tm, LANES), lambda i: (i, Z_START['dn_a'] // LANES)),
                  pl.BlockSpec((nh, tm), lambda i: (0, 0)),
                  pl.BlockSpec((nh, tm), lambda i: (0, 0))],
        out_specs=pl.BlockSpec((2 * nh, tm), lambda i: (0, i)),
        compiler_params=_cparams(("arbitrary",)),
        name="dn_gates",
    )(z, neg_a, dtb)


def _dn_masks(c, upper):
    rows = lax.broadcasted_iota(jnp.int32, (c, c), 0)
    cols = lax.broadcasted_iota(jnp.int32, (c, c), 1)
    levels = []
    s = 1
    while s < c:
        same = (rows ^ cols) < 2 * s
        lo, hi = (rows & s) != 0, (cols & s) == 0
        if upper:
            lo, hi = (rows & s) == 0, (cols & s) != 0
        levels.append(jnp.where(same, jnp.where(lo, jnp.where(hi, 1.0, 0.0), 0.0), 0.0))
        s *= 2
    return dict(eye=rows == cols,
                causal=(rows <= cols) if upper else (rows >= cols),
                causal_t=(rows >= cols) if upper else (rows <= cols),
                strict=(rows < cols) if upper else (rows > cols),
                levels=levels)


def _mm(a, b):
    return jnp.dot(a.astype(BF16), b.astype(BF16), preferred_element_type=F32)


def _mm_nt(a, b):
    return lax.dot_general(a.astype(BF16), b.astype(BF16), _NT, preferred_element_type=F32)


def _dn_chunk_local(q, k, v, g_row, beta_row, masks):
    eye, causal, causal_t = masks['eye'], masks['causal'], masks['causal_t']
    g_col = jnp.sum(jnp.where(eye, g_row, 0.0), axis=1, keepdims=True)
    beta_col = jnp.sum(jnp.where(eye, beta_row, 0.0), axis=1, keepdims=True)
    gc_col = jnp.sum(jnp.where(causal, g_row, 0.0), axis=1, keepdims=True)
    gc_row = jnp.sum(jnp.where(causal_t, g_col, 0.0), axis=0, keepdims=True)
    g_last = jnp.sum(g_row, axis=1, keepdims=True)
    kb = k * beta_col
    return dict(q=q, k=k, kb=kb, masks=masks,
                decay=jnp.where(causal, jnp.exp(jnp.where(causal, gc_col - gc_row, 0.0)), 0.0),
                rhs=jnp.concatenate([v * beta_col, kb * jnp.exp(gc_col)], axis=1),
                q_dec=q * jnp.exp(gc_col),
                k_dec_t=(k * jnp.exp(g_last - gc_col)).T,
                s_decay=jnp.exp(g_last))


def _dn_solve_chunks(chunks):
    for ch in chunks:
        m = ch['masks']
        ch['a'] = jnp.where(m['strict'], _mm_nt(ch['kb'], ch['k']) * ch['decay'], 0.0)
        ch['t'] = jnp.where(m['eye'], 1.0, 0.0) - ch['a'] * m['levels'][0]
    for lvl in range(1, len(chunks[0]['masks']['levels'])):
        for ch in chunks:
            ch['mid'] = _mm(ch['a'] * ch['masks']['levels'][lvl], ch['t'])
        for ch in chunks:
            ch['t'] = ch['t'] - _mm(ch['t'], ch['mid'])
    for ch in chunks:
        sol = _mm(ch['t'], ch['rhs'])
        dv = sol.shape[1] // 2
        ch['u'], ch['w'] = sol[:, :dv], sol[:, dv:]
    for ch in chunks:
        ch['attn'] = _mm_nt(ch['q'], ch['k']) * ch['decay']


def _dn_advance(chunks, states):
    w_s = [_mm(ch['w'], s) for ch, s in zip(chunks, states)]
    q_s = [_mm(ch['q_dec'], s) for ch, s in zip(chunks, states)]
    v_new = [ch['u'] - ws for ch, ws in zip(chunks, w_s)]
    outs = [qs + _mm(ch['attn'], vn) for ch, qs, vn in zip(chunks, q_s, v_new)]
    new_states = [s * ch['s_decay'] + _mm(ch['k_dec_t'], vn) for ch, s, vn in zip(chunks, states, v_new)]
    return outs, new_states


def _dn_chunk_kernel(qf_ref, kf_ref, vf_ref, gf_ref, qb_ref, kb_ref, vb_ref, gb_ref, s0_ref,
                     of_ref, ob_ref, sfin_ref, s_ref):
    hg = pl.program_id(1)
    c = pl.program_id(2)
    hb = s_ref.shape[1]
    hdim = DN_HEAD_DIM

    @pl.when(c == 0)
    def _():
        s_ref[...] = s0_ref[0]

    n_sub = qf_ref.shape[0] // DN_CHUNK
    nh = 2 * DN_HEADS
    dirs = ((qf_ref, kf_ref, vf_ref, gf_ref, of_ref), (qb_ref, kb_ref, vb_ref, gb_ref, ob_ref))
    masks = [_dn_masks(DN_CHUNK, upper=False), _dn_masks(DN_CHUNK, upper=True)]
    chains = [(d, hh) for d in range(2) for hh in range(hb)]
    local = {}
    for d, hh in chains:
        q_ref, k_ref, v_ref, g_ref, _ = dirs[d]
        head = hg * hb + hh
        lanes = slice(hh * hdim, (hh + 1) * hdim)
        g_all = g_ref[pl.ds(d * DN_HEADS + head, 1), :]
        beta_all = g_ref[pl.ds(nh + d * DN_HEADS + head, 1), :]
        for j in range(n_sub):
            tok = slice(j * DN_CHUNK, (j + 1) * DN_CHUNK)
            local[d, hh, j] = _dn_chunk_local(q_ref[tok, lanes], k_ref[tok, lanes], v_ref[tok, lanes],
                                              g_all[:, tok], beta_all[:, tok], masks[d])
    _dn_solve_chunks(list(local.values()))
    states = [s_ref[d, hh] for d, hh in chains]
    for step in range(n_sub):
        subs = [step if d == 0 else n_sub - 1 - step for d, _ in chains]
        outs, states = _dn_advance([local[d, hh, j] for (d, hh), j in zip(chains, subs)], states)
        for (d, hh), j, o in zip(chains, subs, outs):
            dirs[d][4][j * DN_CHUNK:(j + 1) * DN_CHUNK, hh * hdim:(hh + 1) * hdim] = o
    for (d, hh), s in zip(chains, states):
        s_ref[d, hh] = s
    sfin_ref[0] = s_ref[...]


def dn_chunk(qkv, gates, s0, tok0, seq_len, blk=128, hb=8):
    b = s0.shape[0]
    nb = seq_len // blk
    hdim = DN_HEAD_DIM
    n_hg = DN_HEADS // hb
    fwd = lambda bi, c: (tok0 + bi * seq_len) // blk + c
    bwd = lambda bi, c: (tok0 + bi * seq_len) // blk + nb - 1 - c

    def col_spec(pos, o):
        return pl.BlockSpec((blk, hb * hdim), lambda bi, hg, c: (pos(bi, c), o * n_hg + hg))

    def specs(pos):
        return [col_spec(pos, o) for o in range(3)] + [
            pl.BlockSpec((4 * DN_HEADS, blk), lambda bi, hg, c: (0, pos(bi, c)))]

    st_spec = pl.BlockSpec((1, 2, hb, hdim, hdim), lambda bi, hg, c: (bi, 0, hg, 0, 0))
    o_shape = jax.ShapeDtypeStruct((b * seq_len, GROUP_W), F32)
    return pl.pallas_call(
        _dn_chunk_kernel,
        out_shape=(o_shape, o_shape, jax.ShapeDtypeStruct(s0.shape, F32)),
        grid=(b, n_hg, nb),
        in_specs=specs(fwd) + specs(bwd) + [st_spec],
        out_specs=(pl.BlockSpec((blk, hb * hdim), lambda bi, hg, c: (bi * nb + c, hg)),
                   pl.BlockSpec((blk, hb * hdim), lambda bi, hg, c: (bi * nb + nb - 1 - c, hg)),
                   st_spec),
        scratch_shapes=[pltpu.VMEM((2, hb, hdim, hdim), F32)],
        compiler_params=_cparams(("arbitrary", "arbitrary", "arbitrary")),
        name="dn_chunk",
    )(qkv, qkv, qkv, gates, qkv, qkv, qkv, gates, s0)


def _dn_post_kernel(ofc_ref, ofl_ref, obc_ref, obl_ref, gate_ref, g_ref, o_ref, *, nc):
    both = _region_pick(ofc_ref, ofl_ref, nc) + _region_pick(obc_ref, obl_ref, nc)
    for hh in range(DN_HEADS):
        sl = slice(hh * DN_HEAD_DIM, (hh + 1) * DN_HEAD_DIM)
        o = _head_rmsnorm(both[:, sl], g_ref[...])
        o_ref[:, sl] = (o * _silu(gate_ref[:, sl])).astype(o_ref.dtype)


def dn_post(o_f, o_b, z, norm_g, n_ctx_tok, tm=512):
    t = z.shape[0]
    blk = pl.BlockSpec((tm, GROUP_W), lambda i: (i, 0))
    pair = _region_specs(tm, GROUP_W, n_ctx_tok)
    return pl.pallas_call(
        functools.partial(_dn_post_kernel, nc=n_ctx_tok // tm),
        out_shape=jax.ShapeDtypeStruct((t, GROUP_W), BF16),
        grid=(t // tm,),
        in_specs=[*pair, *pair, pl.BlockSpec((tm, GROUP_W), lambda i: (i, Z_START['dn_gate'] // GROUP_W)),
                  pl.BlockSpec((1, DN_HEAD_DIM), lambda i: (0, 0))],
        out_specs=blk,
        compiler_params=_cparams(("arbitrary",)),
        name="dn_post",
    )(*o_f, *o_b, z, norm_g.reshape(1, -1))


def _rope_tables(length, dim):
    n_rows = length // GRID_W
    row = jnp.repeat(jnp.arange(n_rows), GRID_W).astype(F32)
    col = jnp.tile(jnp.arange(GRID_W), n_rows).astype(F32)
    quarter = dim // 4
    freqs = ROPE_THETA ** (-jnp.arange(quarter, dtype=F32) / quarter)
    ang = jnp.concatenate([row[:, None] * freqs, col[:, None] * freqs], axis=-1)
    return jnp.cos(ang), jnp.sin(ang)


def _permute_w_in(w):
    cols = [w[:, Z_REF_START[n]:Z_REF_START[n] + Z_WIDTH[n]] for n in _Z_ORDER]
    used = sum(Z_WIDTH[n] for n in _Z_ORDER)
    cols.append(jnp.zeros((w.shape[0], Z_COLS - used), w.dtype))
    return jnp.concatenate(cols, axis=1).astype(BF16)


def _pad_keys(keys):
    h, two, n, half = keys.shape
    z = jnp.zeros((h, n, half), keys.dtype)
    k0 = jnp.concatenate([keys[:, 0], z], axis=-1)
    k1 = jnp.concatenate([z, keys[:, 1]], axis=-1)
    return jnp.stack([k0, k1], axis=1).reshape(2 * h, n, 2 * half).astype(BF16)


def kernel(x_prompt, x_sample, c, cache_gqa_k, cache_gqa_v, cache_diff_k, cache_diff_v, state_s5_re, state_s5_im, state_delta, c_ctx, w_mod, b_mod, norm1_g, norm2_g, w_in, s5_lambda_re, s5_lambda_im, s5_log_dt, s5_b_re, s5_b_im, s5_c_re, s5_c_im, s5_d, s5_w_glu, dn_conv_w, dn_a_log, dn_dt_bias, dn_norm_g, gqa_q_norm, gqa_k_norm, diff_lambda, diff_subln_g, w_out, peer_w_q, peer_keys, peer_u, peer_v, final_norm_g):
    batch, seq, d = x_prompt.shape
    dec_batch, lat_len, _ = x_sample.shape
    depth = w_in.shape[0]
    n_ctx_tok = batch * seq
    n_lat_tok = dec_batch * lat_len
    tab_g = _rope_lane_tables(lat_len, GQA_HEAD_DIM, n_ctx_tok, dec_batch)
    tab_d = _rope_lane_tables(lat_len, DIFF_QK_DIM, n_ctx_tok, dec_batch)
    kv_w = GQA_KV_HEADS * GQA_HEAD_DIM

    def zcol(z, name, lo, hi):
        return z[lo:hi, Z_START[name]:Z_START[name] + Z_WIDTH[name]]

    def with_cache(cache, new, width):
        return jnp.concatenate([cache.reshape(dec_batch, -1, width), new.reshape(dec_batch, lat_len, width)],
                               axis=1).astype(BF16)

    cond = jnp.concatenate([c_ctx[None], c, jnp.zeros((8 - 1 - dec_batch, d), F32)], axis=0)
    mod_all = modulation(cond, w_mod, b_mod).reshape(depth, 8, N_MOD, d)

    x = jnp.concatenate([x_prompt.reshape(n_ctx_tok, d), x_sample.reshape(dec_batch * lat_len, d)], axis=0)
    new_state = [[] for _ in range(7)]
    for l in range(depth):
        mod = mod_all[l]
        z = in_projection(x, norm1_g[l], mod, _permute_w_in(w_in[l]), n_ctx_tok, lat_len)

        wb, wc, coef = _s5_tables(s5_lambda_re[l], s5_lambda_im[l], s5_log_dt[l],
                                  s5_b_re[l], s5_b_im[l], s5_c_re[l], s5_c_im[l])
        h0_ctx = jnp.zeros((batch, 2, S5_BLOCKS, 1, 2 * S5_BLK_STATE), F32)
        yf_c, yb_c, s5_fin = s5_scan(z, wb, wc, coef, h0_ctx, 0, seq, tc=seq)
        yf_l, yb_l, _ = s5_scan(z, wb, wc, coef, _s5_state_in(state_s5_re[:, l], state_s5_im[:, l]),
                                n_ctx_tok, lat_len, tc=256)
        y_a = s5_glu((yf_c, yf_l), (yb_c, yb_l), z, s5_d[l], s5_w_glu[l].astype(BF16), n_ctx_tok)
        s5_re, s5_im = _s5_state_out(s5_fin)

        qkv = dn_conv(z, dn_conv_w[l], n_ctx_tok, seq, lat_len)
        gates = dn_gates(z, dn_a_log[l], dn_dt_bias[l])
        s0_ctx = jnp.zeros((batch, 2, DN_HEADS, DN_HEAD_DIM, DN_HEAD_DIM), F32)
        of_c, ob_c, delta = dn_chunk(qkv, gates, s0_ctx, 0, seq)
        of_l, ob_l, _ = dn_chunk(qkv, gates, state_delta[:, l], n_ctx_tok, lat_len)
        y_b = dn_post((of_c, of_l), (ob_c, ob_l), z, dn_norm_g[l], n_ctx_tok)

        kg, kd = key_prep(z, gqa_k_norm[l], (*tab_g, *tab_d))
        gk_ctx, gv_ctx = kg[:n_ctx_tok], zcol(z, 'gv', 0, n_ctx_tok)
        fk_ctx, fv_ctx = kd[:n_ctx_tok], zcol(z, 'fv', 0, n_ctx_tok)
        yc_c = gqa_attention(z, gqa_q_norm[l], *tab_g, gk_ctx.reshape(batch, seq, kv_w).astype(BF16),
                             gv_ctx.reshape(batch, seq, kv_w).astype(BF16), 0, seq, tq=seq)
        yc_l = gqa_attention(z, gqa_q_norm[l], *tab_g, with_cache(cache_gqa_k[:, l], kg[n_ctx_tok:], kv_w),
                             with_cache(cache_gqa_v[:, l], zcol(z, 'gv', n_ctx_tok, None), kv_w),
                             n_ctx_tok, lat_len, tq=128)
        yd_c = diff_attention(z, *tab_d, fk_ctx.reshape(batch, seq, GROUP_W).astype(BF16),
                              fv_ctx.reshape(batch, seq, GROUP_W).astype(BF16),
                              diff_lambda[l], diff_subln_g[l], l, 0, seq, tq=seq)
        yd_l = diff_attention(z, *tab_d, with_cache(cache_diff_k[:, l], kd[n_ctx_tok:], GROUP_W),
                              with_cache(cache_diff_v[:, l], zcol(z, 'fv', n_ctx_tok, None), GROUP_W),
                              diff_lambda[l], diff_subln_g[l], l, n_ctx_tok, lat_len, tq=256)
        ctx_out = (gk_ctx.reshape(batch, seq, GQA_KV_HEADS, GQA_HEAD_DIM),
                   gv_ctx.reshape(batch, seq, GQA_KV_HEADS, GQA_HEAD_DIM),
                   fk_ctx.reshape(batch, seq, DIFF_HEADS, 2, DIFF_QK_DIM),
                   fv_ctx.reshape(batch, seq, DIFF_HEADS, DIFF_V_DIM),
                   s5_re, s5_im, delta)

        x = out_projection(y_a, y_b, (yc_c, yc_l), (yd_c, yd_l), w_out[l].astype(BF16), x, mod,
                           n_ctx_tok, lat_len)
        h2, s2, e2, thr, e1 = peer_query(x, norm2_g[l], mod, peer_w_q[l].astype(BF16), _pad_keys(peer_keys[l]),
                                         n_ctx_tok, lat_len)
        peer_t = peer_main(h2, table_bf16(peer_u, l, transpose=False), table_bf16(peer_v, l, transpose=True),
                           s2, e2, thr, e1)
        residual = functools.partial(peer_residual, x, peer_t, mod, final_norm_g, n_ctx_tok, lat_len)
        if l < depth - 1:
            x = residual(False, 0, n_ctx_tok + n_lat_tok)
        else:
            y_prompt = residual(True, 0, n_ctx_tok).reshape(batch, seq, d)
            y_sample = residual(True, n_ctx_tok, n_lat_tok).reshape(dec_batch, lat_len, d)
        for acc, val in zip(new_state, ctx_out):
            acc.append(val)
    return (y_prompt, y_sample) + tuple(jnp.stack(s, axis=1) for s in new_state)
```
